```python
import math
import jax, jax.numpy as jnp
from jax import lax
import numpy as np

D_MODEL = 1024
BATCH = 2
SEQ = 8192
DEPTH = 1

CHUNK = 64
D_MIX = D_MODEL
D_GMLP = D_MIX // 2
D_DIFF = D_MIX - D_GMLP
GM_GROUPS = 4
GM_GROUP_DIM = D_GMLP // GM_GROUPS
GM_WINDOW = 128
DIFF_HEADS = 4
DIFF_HEAD_DIM = D_DIFF // (2 * DIFF_HEADS)
DIFF_V_DIM = 2 * DIFF_HEAD_DIM
Q_BLOCK = 128
ROPE_THETA = 10000.0
EPS = 1e-6
MAX_OFFSET = 4096
SPLITS = [D_GMLP, 2 * D_GMLP, 3 * D_GMLP,
          3 * D_GMLP + D_DIFF, 3 * D_GMLP + 2 * D_DIFF, 3 * D_GMLP + 3 * D_DIFF]
D_IN = 3 * D_GMLP + 4 * D_DIFF

kernel_name = "hybrid_gmlp_diffattn_block"


def rms_norm(x, g):
    xf = x.astype(jnp.float32)
    y = xf * lax.rsqrt(jnp.mean(xf * xf, axis=-1, keepdims=True) + EPS)
    return (y * g.astype(jnp.float32)).astype(x.dtype)


def layer_norm(x, g, b):
    xf = x.astype(jnp.float32)
    mu = jnp.mean(xf, axis=-1, keepdims=True)
    xc = xf - mu
    y = xc * lax.rsqrt(jnp.mean(xc * xc, axis=-1, keepdims=True) + EPS)
    return (y * g.astype(jnp.float32) + b.astype(jnp.float32)).astype(x.dtype)


def rope(x, positions):
    d = x.shape[-1]
    half = d // 2
    inv = ROPE_THETA ** (-jnp.arange(half, dtype=jnp.float32) * 2.0 / d)
    ang = positions.astype(jnp.float32)[:, :, None] * inv
    cos = jnp.cos(ang)[:, :, None, None, :]
    sin = jnp.sin(ang)[:, :, None, None, :]
    xf = x.astype(jnp.float32)
    x1, x2 = xf[..., :half], xf[..., half:]
    out = jnp.concatenate([x1 * cos - x2 * sin, x2 * cos + x1 * sin], axis=-1)
    return out.astype(x.dtype)


def gmlp_branch(u, v, z, ln_g, ln_b, ws, bs):
    B, S, _ = u.shape
    u = jax.nn.gelu(u, approximate=False)
    v = jax.nn.gelu(v, approximate=False)
    v = layer_norm(v.reshape(B, S, GM_GROUPS, GM_GROUP_DIM),
                   ln_g.reshape(GM_GROUPS, GM_GROUP_DIM), ln_b.reshape(GM_GROUPS, GM_GROUP_DIM))
    v = v.reshape(B, S // GM_WINDOW, GM_WINDOW, GM_GROUPS, GM_GROUP_DIM)
    pos_chunk = jnp.arange(GM_WINDOW) // CHUNK
    mask = pos_chunk[:, None] >= pos_chunk[None, :]
    ws_m = jnp.where(mask[None], ws, jnp.zeros((), ws.dtype))
    sv = jnp.einsum('gij,bnjgc->bnigc', ws_m, v) + bs.T[None, None, :, :, None]
    sv = sv.reshape(B, S, D_GMLP)
    return u * sv * jax.nn.silu(z)


def diff_attention(q, k, v, z, positions, lq1, lk1, lq2, lk2, subln_g, lam_init):
    B, S, _ = q.shape
    q = rope(q.reshape(B, S, DIFF_HEADS, 2, DIFF_HEAD_DIM), positions) * (DIFF_HEAD_DIM ** -0.5)
    k = rope(k.reshape(B, S, DIFF_HEADS, 2, DIFF_HEAD_DIM), positions)
    v = v.reshape(B, S, DIFF_HEADS, DIFF_V_DIM)
    lam = (jnp.exp(jnp.sum(lq1.astype(jnp.float32) * lk1.astype(jnp.float32)))
           - jnp.exp(jnp.sum(lq2.astype(jnp.float32) * lk2.astype(jnp.float32))) + lam_init)
    nqb = S // Q_BLOCK
    q_blocks = q.reshape(B, nqb, Q_BLOCK, DIFF_HEADS, 2, DIFF_HEAD_DIM).transpose(1, 0, 2, 3, 4, 5)
    key_chunk = jnp.arange(S) // CHUNK

    def block(args):
        qb, i = args
        q_chunk = (i * Q_BLOCK + jnp.arange(Q_BLOCK)) // CHUNK
        allowed = key_chunk[None, :] <= q_chunk[:, None]
        s = jnp.einsum('bqhtd,bkhtd->bhtqk', qb, k).astype(jnp.float32)
        s = jnp.where(allowed, s, -jnp.inf)
        p = jax.nn.softmax(s, axis=-1)
        a = p[:, :, 0] - lam * p[:, :, 1]
        return jnp.einsum('bhqk,bkhe->bqhe', a.astype(v.dtype), v)

    out = lax.map(block, (q_blocks, jnp.arange(nqb)))
    out = out.transpose(1, 0, 2, 3, 4).reshape(B, S, DIFF_HEADS, DIFF_V_DIM)
    out = rms_norm(out, subln_g) * (1.0 - lam_init)
    return out.reshape(B, S, D_DIFF) * jax.nn.silu(z)


def setup_inputs(seed: int = 0) -> dict:
    key = jax.random.key(seed)
    ks = jax.random.split(key, 20)
    f = jnp.float32
    x = jax.random.normal(ks[0], (BATCH, SEQ, D_MODEL), f)
    c = jax.random.normal(ks[1], (BATCH, D_MODEL), f)
    offset = jax.random.randint(ks[2], (BATCH, 1), 0, MAX_OFFSET, dtype=jnp.int32)
    positions = (offset + jnp.arange(SEQ, dtype=jnp.int32)[None, :]).astype(jnp.int32)
    w_ada = jax.random.normal(ks[3], (DEPTH, D_MODEL, 3 * D_MODEL), f) * (0.2 * D_MODEL ** -0.5)
    b_ada = jax.random.normal(ks[4], (DEPTH, 3 * D_MODEL), f) * 0.02
    norm_g = 1.0 + 0.02 * jax.random.normal(ks[5], (DEPTH, D_MODEL), f)
    w_in = jax.random.normal(ks[6], (DEPTH, D_MODEL, D_IN), f) * (D_MODEL ** -0.5)
    gm_ln_g = 1.0 + 0.02 * jax.random.normal(ks[7], (DEPTH, D_GMLP), f)
    gm_ln_b = 0.02 * jax.random.normal(ks[8], (DEPTH, D_GMLP), f)
    gm_ws = jax.random.normal(ks[9], (DEPTH, GM_GROUPS, GM_WINDOW, GM_WINDOW), f) * (GM_WINDOW ** -0.5)
    gm_bs = 1.0 + 0.02 * jax.random.normal(ks[10], (DEPTH, GM_GROUPS, GM_WINDOW), f)
    lam_q1 = 0.1 * jax.random.normal(ks[11], (DEPTH, DIFF_HEAD_DIM), f)
    lam_k1 = 0.1 * jax.random.normal(ks[12], (DEPTH, DIFF_HEAD_DIM), f)
    lam_q2 = 0.1 * jax.random.normal(ks[13], (DEPTH, DIFF_HEAD_DIM), f)
    lam_k2 = 0.1 * jax.random.normal(ks[14], (DEPTH, DIFF_HEAD_DIM), f)
    diff_subln_g = 1.0 + 0.02 * jax.random.normal(ks[15], (DEPTH, DIFF_V_DIM), f)
    w_out = jax.random.normal(ks[16], (DEPTH, D_MIX, D_MODEL), f) * (D_MIX ** -0.5)
    final_g = 1.0 + 0.02 * jax.random.normal(ks[17], (D_MODEL,), f)
    return {"x": x, "c": c, "positions": positions, "w_ada": w_ada, "b_ada": b_ada,
            "norm_g": norm_g, "w_in": w_in, "gm_ln_g": gm_ln_g, "gm_ln_b": gm_ln_b,
            "gm_ws": gm_ws, "gm_bs": gm_bs, "lam_q1": lam_q1, "lam_k1": lam_k1,
            "lam_q2": lam_q2, "lam_k2": lam_k2, "diff_subln_g": diff_subln_g,
            "w_out": w_out, "final_g": final_g}


def reference(x, c, positions, w_ada, b_ada, norm_g, w_in, gm_ln_g, gm_ln_b, gm_ws, gm_bs,
              lam_q1, lam_k1, lam_q2, lam_k2, diff_subln_g, w_out, final_g):
    cs = jax.nn.silu(c)
    for l in range(DEPTH):
        mod = cs @ w_ada[l] + b_ada[l]
        shift, scale, gate = jnp.split(mod, 3, axis=-1)
        h = rms_norm(x, norm_g[l]) * (1.0 + scale[:, None, :]) + shift[:, None, :]
        proj = h @ w_in[l]
        u, v, z_a, q, k, v_d, z_b = jnp.split(proj, SPLITS, axis=-1)
        y_a = gmlp_branch(u, v, z_a, gm_ln_g[l], gm_ln_b[l], gm_ws[l], gm_bs[l])
        lam_init = 0.8 - 0.6 * math.exp(-0.3 * l)
        y_b = diff_attention(q, k, v_d, z_b, positions, lam_q1[l], lam_k1[l], lam_q2[l], lam_k2[l],
                             diff_subln_g[l], lam_init)
        y = jnp.concatenate([y_a, y_b], axis=-1) @ w_out[l]
        x = x + gate[:, None, :] * y
    return rms_norm(x, final_g)
```

```python
import functools
import math

import jax
import jax.numpy as jnp
from jax import lax
from jax.experimental import pallas as pl
from jax.experimental.pallas import tpu as pltpu

F32 = jnp.float32
BF16 = jnp.bfloat16

EPS = 1e-6
CHUNK = 64
GM_GROUPS = 4
GM_GROUP_DIM = 128
GM_WINDOW = 128
DIFF_HEADS = 4
DIFF_HEAD_DIM = 64
DIFF_V_DIM = 128
ROPE_THETA = 10000.0
SEC = 512
N_SEC = 7
LOG2E = 1.4426950408889634

LANES = 128
VMEM_LIMIT = 48 * 1024 * 1024

TM_PROJ = 512
T_ATT = 256


def _silu(x):
    return x * (1.0 / (1.0 + jnp.exp(-x)))


def _gelu(x):
    return 0.5 * x * (1.0 + lax.erf(x * (1.0 / math.sqrt(2.0))))


def _mod_kernel(c_ref, w_ref, b_ref, o_ref):
    cs = _silu(c_ref[...]).astype(BF16)
    w = w_ref[...].astype(BF16)
    o_ref[...] = jnp.dot(cs, w, preferred_element_type=F32) + b_ref[...]


def _modulation(c8, w_ada, b_ada):
    d, n = w_ada.shape
    tn = 1024
    return pl.pallas_call(
        _mod_kernel,
        grid=(n // tn,),
        in_specs=[
            pl.BlockSpec((8, d), lambda j: (0, 0)),
            pl.BlockSpec((d, tn), lambda j: (0, j)),
            pl.BlockSpec((1, tn), lambda j: (0, j)),
        ],
        out_specs=pl.BlockSpec((8, tn), lambda j: (0, j)),
        out_shape=jax.ShapeDtypeStruct((8, n), F32),
        compiler_params=pltpu.CompilerParams(
            dimension_semantics=("arbitrary",), vmem_limit_bytes=VMEM_LIMIT),
        name="adaln_mod",
    )(c8, w_ada, b_ada)


def _rope_heads(t, cos, sin_signed, lane_lo):
    outs = []
    for hh in range(SEC // LANES):
        xh = t[:, hh * LANES:(hh + 1) * LANES]
        partner = jnp.where(lane_lo, pltpu.roll(xh, LANES - 32, 1), pltpu.roll(xh, 32, 1))
        outs.append(xh * cos + partner * sin_signed)
    return outs


def _proj_kernel(x_ref, pos_ref, scale_ref, shift_ref, ng_ref, w_ref, inv_ref,
                 lng_ref, lnb_ref, ws_ref, bs_ref,
                 ya_ref, q_ref, k_ref, v_ref, zb_ref, vln_ref, sv_ref):
    tm = x_ref.shape[1]
    x = x_ref[0]
    ms = jnp.mean(x * x, axis=-1, keepdims=True)
    xn = (x * lax.rsqrt(ms + EPS)) * ng_ref[...]
    h = xn * (1.0 + scale_ref[0]) + shift_ref[0]
    hb = h.astype(BF16)

    def proj(sec):
        return jnp.dot(hb, w_ref[:, sec * SEC:(sec + 1) * SEC], preferred_element_type=F32)

    v = _gelu(proj(1))
    for g in range(GM_GROUPS):
        vg = v[:, g * LANES:(g + 1) * LANES]
        mu = jnp.mean(vg, axis=-1, keepdims=True)
        vc = vg - mu
        var = jnp.mean(vc * vc, axis=-1, keepdims=True)
        vn = vc * lax.rsqrt(var + EPS)
        vn = vn * lng_ref[:, g * LANES:(g + 1) * LANES] + lnb_ref[:, g * LANES:(g + 1) * LANES]
        vln_ref[:, g * LANES:(g + 1) * LANES] = vn.astype(BF16)

    row_c = lax.broadcasted_iota(jnp.int32, (GM_WINDOW, GM_WINDOW), 0) // CHUNK
    col_c = lax.broadcasted_iota(jnp.int32, (GM_WINDOW, GM_WINDOW), 1) // CHUNK
    tri = row_c >= col_c
    for g in range(GM_GROUPS):
        wsm = jnp.where(tri, ws_ref[g], 0.0).astype(BF16)
        bias = bs_ref[g]
        for w in range(tm // GM_WINDOW):
            rows = slice(w * GM_WINDOW, (w + 1) * GM_WINDOW)
            cols = slice(g * LANES, (g + 1) * LANES)
            sv_ref[rows, cols] = jnp.dot(wsm, vln_ref[rows, cols],
                                         preferred_element_type=F32) + bias

    t = _gelu(proj(0)) * sv_ref[...]
    ya_ref[0] = (t * _silu(proj(2))).astype(BF16)

    ang = pos_ref[0].astype(F32) * inv_ref[...]
    cos = jnp.cos(ang)
    sin = jnp.sin(ang)
    lane = lax.broadcasted_iota(jnp.int32, (1, LANES), 1)
    lane_lo = (lane % DIFF_HEAD_DIM) < (DIFF_HEAD_DIM // 2)
    sin_signed = jnp.where(lane_lo, -sin, sin)

    qscale = (DIFF_HEAD_DIM ** -0.5) * LOG2E
    for hh, qh in enumerate(_rope_heads(proj(3), cos, sin_signed, lane_lo)):
        q_ref[0, hh] = (qh * qscale).astype(BF16)
    for hh, kh in enumerate(_rope_heads(proj(4), cos, sin_signed, lane_lo)):
        k_ref[0, hh] = kh.astype(BF16)
    vd = proj(5)
    for hh in range(DIFF_HEADS):
        v_ref[0, hh] = vd[:, hh * LANES:(hh + 1) * LANES].astype(BF16)
    zb_ref[0] = _silu(proj(6)).astype(BF16)


def _projection(x, pos3, scale, shift, norm_g, w_in, inv128, ln_g, ln_b, ws, bs3):
    b, s, d = x.shape
    tm = TM_PROJ
    head_shape = jax.ShapeDtypeStruct((b, DIFF_HEADS, s, LANES), BF16)
    row_shape = jax.ShapeDtypeStruct((b, s, SEC), BF16)
    head_spec = pl.BlockSpec((1, DIFF_HEADS, tm, LANES), lambda bi, i: (bi, 0, i, 0))
    row_spec = pl.BlockSpec((1, tm, SEC), lambda bi, i: (bi, i, 0))
    const2 = lambda bi, i: (0, 0)
    const3 = lambda bi, i: (0, 0, 0)
    return pl.pallas_call(
        _proj_kernel,
        grid=(b, s // tm),
        in_specs=[
            pl.BlockSpec((1, tm, d), lambda bi, i: (bi, i, 0)),
            pl.BlockSpec((1, tm, 1), lambda bi, i: (bi, i, 0)),
            pl.BlockSpec((1, 1, d), lambda bi, i: (bi, 0, 0)),
            pl.BlockSpec((1, 1, d), lambda bi, i: (bi, 0, 0)),
            pl.BlockSpec((1, d), const2),
            pl.BlockSpec((d, N_SEC * SEC), const2),
            pl.BlockSpec((1, LANES), const2),
            pl.BlockSpec((1, SEC), const2),
            pl.BlockSpec((1, SEC), const2),
            pl.BlockSpec((GM_GROUPS, GM_WINDOW, GM_WINDOW), const3),
            pl.BlockSpec((GM_GROUPS, GM_WINDOW, 1), const3),
        ],
        out_specs=[row_spec, head_spec, head_spec, head_spec, row_spec],
        out_shape=[row_shape, head_shape, head_shape, head_shape, row_shape],
        scratch_shapes=[pltpu.VMEM((tm, SEC), BF16), pltpu.VMEM((tm, SEC), F32)],
        compiler_params=pltpu.CompilerParams(
            dimension_semantics=("arbitrary", "arbitrary"), vmem_limit_bytes=VMEM_LIMIT),
        name="in_proj_gmlp_rope",
    )(x, pos3, scale, shift, norm_g, w_in, inv128, ln_g, ln_b, ws, bs3)


def _attn_kernel(lam_init, q_ref, k_ref, v_ref, zb_ref, g_ref, lq1_ref, lk1_ref, lq2_ref,
                 lk2_ref, o_ref, acc_ref, m_ref, l_ref):
    s_len = q_ref.shape[2]
    t = T_ATT
    nq = s_len // t

    lam = (jnp.exp(jnp.sum(lq1_ref[...] * lk1_ref[...], axis=-1, keepdims=True))
           - jnp.exp(jnp.sum(lq2_ref[...] * lk2_ref[...], axis=-1, keepdims=True))
           + lam_init)

    lane = lax.broadcasted_iota(jnp.int32, (1, LANES), 1)
    first_comp = lane < DIFF_HEAD_DIM
    row_c = lax.broadcasted_iota(jnp.int32, (2 * t, t), 0) % t // CHUNK
    col_c = lax.broadcasted_iota(jnp.int32, (2 * t, t), 1) // CHUNK
    diag_allowed = col_c <= row_c

    def q_block(qi, carry):
        r0 = pl.multiple_of(qi * t, t)
        qb = q_ref[0, 0, pl.ds(r0, t), :]
        zero = jnp.zeros_like(qb)
        q2 = jnp.concatenate([jnp.where(first_comp, qb, zero),
                              jnp.where(first_comp, zero, qb)], axis=0)
        m_ref[...] = jnp.full(m_ref.shape, -jnp.inf, F32)
        l_ref[...] = jnp.zeros(l_ref.shape, F32)
        acc_ref[...] = jnp.zeros(acc_ref.shape, F32)

        def tile(j, masked):
            c0 = pl.multiple_of(j * t, t)
            kt = k_ref[0, 0, pl.ds(c0, t), :]
            vt = v_ref[0, 0, pl.ds(c0, t), :]
            s = lax.dot_general(q2, kt, (((1,), (1,)), ((), ())),
                                preferred_element_type=F32)
            if masked:
                s = jnp.where(diag_allowed, s, -jnp.inf)
            m_prev = m_ref[...]
            m_new = jnp.maximum(m_prev, jnp.max(s, axis=1, keepdims=True))
            alpha = jnp.exp2(m_prev - m_new)
            p = jnp.exp2(s - m_new)
            l_ref[...] = alpha * l_ref[...] + jnp.sum(p, axis=1, keepdims=True)
            acc_ref[...] = alpha * acc_ref[...] + jnp.dot(
                p.astype(BF16), vt, preferred_element_type=F32)
            m_ref[...] = m_new

        def full_tile(j, c):
            tile(j, False)
            return c

        lax.fori_loop(0, qi, full_tile, 0)
        tile(qi, True)

        o_all = acc_ref[...] / l_ref[...]
        o = o_all[:t] - lam * o_all[t:]
        ms = jnp.mean(o * o, axis=-1, keepdims=True)
        on = (o * lax.rsqrt(ms + EPS)) * g_ref[...]
        on = on * (1.0 - lam_init)
        o_ref[0, pl.ds(r0, t), :] = (on * zb_ref[0, pl.ds(r0, t), :].astype(F32)).astype(BF16)
        return carry

    lax.fori_loop(0, nq, q_block, 0)


def _attention(q, k, v, zb, subln_g, lq1, lk1, lq2, lk2, lam_init):
    b, h, s, _ = q.shape
    t = T_ATT
    head_spec = pl.BlockSpec((1, 1, s, LANES), lambda bi, hi: (bi, hi, 0, 0))
    col_spec = pl.BlockSpec((1, s, LANES), lambda bi, hi: (bi, 0, hi))
    vec64 = pl.BlockSpec((1, DIFF_HEAD_DIM), lambda bi, hi: (0, 0))
    return pl.pallas_call(
        functools.partial(_attn_kernel, lam_init),
        grid=(b, h),
        in_specs=[head_spec, head_spec, head_spec, col_spec,
                  pl.BlockSpec((1, LANES), lambda bi, hi: (0, 0)),
                  vec64, vec64, vec64, vec64],
        out_specs=col_spec,
        out_shape=jax.ShapeDtypeStruct((b, s, h * LANES), BF16),
        scratch_shapes=[pltpu.VMEM((2 * t, LANES), F32),
                        pltpu.VMEM((2 * t, 1), F32),
                        pltpu.VMEM((2 * t, 1), F32)],
        compiler_params=pltpu.CompilerParams(
            dimension_semantics=("arbitrary", "arbitrary"), vmem_limit_bytes=VMEM_LIMIT),
        name="diff_attention",
    )(q, k, v, zb, subln_g, lq1, lk1, lq2, lk2)


def _out_kernel(ya_ref, yb_ref, x_ref, gate_ref, w_ref, fg_ref, o_ref):
    y = jnp.dot(ya_ref[0], w_ref[:SEC, :], preferred_element_type=F32)
    y = y + jnp.dot(yb_ref[0], w_ref[SEC:, :], preferred_element_type=F32)
    xo = x_ref[0] + gate_ref[0] * y
    ms = jnp.mean(xo * xo, axis=-1, keepdims=True)
    o_ref[0] = (xo * lax.rsqrt(ms + EPS)) * fg_ref[...]


def _out_projection(ya, yb, x, gate, w_out, final_g):
    b, s, d = x.shape
    tm = TM_PROJ
    return pl.pallas_call(
        _out_kernel,
        grid=(b, s // tm),
        in_specs=[
            pl.BlockSpec((1, tm, SEC), lambda bi, i: (bi, i, 0)),
            pl.BlockSpec((1, tm, SEC), lambda bi, i: (bi, i, 0)),
            pl.BlockSpec((1, tm, d), lambda bi, i: (bi, i, 0)),
            pl.BlockSpec((1, 1, d), lambda bi, i: (bi, 0, 0)),
            pl.BlockSpec((2 * SEC, d), lambda bi, i: (0, 0)),
            pl.BlockSpec((1, d), lambda bi, i: (0, 0)),
        ],
        out_specs=pl.BlockSpec((1, tm, d), lambda bi, i: (bi, i, 0)),
        out_shape=jax.ShapeDtypeStruct((b, s, d), F32),
        compiler_params=pltpu.CompilerParams(
            dimension_semantics=("arbitrary", "arbitrary"), vmem_limit_bytes=VMEM_LIMIT),
        name="out_proj_residual_norm",
    )(ya, yb, x, gate, w_out, final_g)


def kernel(x, c, positions, w_ada, b_ada, norm_g, w_in, gm_ln_g, gm_ln_b, gm_ws, gm_bs,
           lam_q1, lam_k1, lam_q2, lam_k2, diff_subln_g, w_out, final_g):
    b, s, d = x.shape
    depth = w_ada.shape[0]
    half = DIFF_HEAD_DIM // 2
    inv = ROPE_THETA ** (-jnp.arange(half, dtype=F32) * 2.0 / DIFF_HEAD_DIM)
    inv128 = jnp.tile(inv, LANES // half)[None, :]
    pos3 = positions.reshape(b, s, 1)
    c8 = jnp.zeros((8, d), F32).at[:b].set(c)
    assert depth == 1
    l = 0
    mod = _modulation(c8, w_ada[l], b_ada[l][None, :])[:b]
    shift = mod[:, None, :d]
    scale = mod[:, None, d:2 * d]
    gate = mod[:, None, 2 * d:]
    ya, q, k, v, zb = _projection(
        x, pos3, scale, shift, norm_g[l][None, :], w_in[l].astype(BF16), inv128,
        gm_ln_g[l][None, :], gm_ln_b[l][None, :], gm_ws[l], gm_bs[l][:, :, None])
    lam_init = 0.8 - 0.6 * math.exp(-0.3 * l)
    yb = _attention(q, k, v, zb, diff_subln_g[l][None, :], lam_q1[l][None, :],
                    lam_k1[l][None, :], lam_q2[l][None, :], lam_k2[l][None, :], lam_init)
    return _out_projection(ya, yb, x, gate, w_out[l].astype(BF16), final_g[None, :])
```

```python
import functools
import math

import jax
import jax.numpy as jnp
from jax import lax
from jax.experimental import pallas as pl
from jax.experimental.pallas import tpu as pltpu

F32 = jnp.float32
BF16 = jnp.bfloat16

EPS = 1e-6
CHUNK = 64
GM_GROUPS = 4
GM_GROUP_DIM = 128
GM_WINDOW = 128
DIFF_HEADS = 4
DIFF_HEAD_DIM = 64
DIFF_V_DIM = 128
ROPE_THETA = 10000.0
SEC = 512
N_SEC = 7
LOG2E = 1.4426950408889634

LANES = 128
VMEM_LIMIT = 48 * 1024 * 1024

TM_PROJ = 512
TQ_ATT = 256
TK_ATT = 256
G_ATT = 4
SUPER_ATT = TQ_ATT * G_ATT
FULL_TILES_PER_ITER = 4
assert SUPER_ATT % TK_ATT == 0 and TQ_ATT == TK_ATT and TK_ATT % CHUNK == 0
assert (SUPER_ATT // TK_ATT) % FULL_TILES_PER_ITER == 0


def _silu(x):
    return x * (1.0 / (1.0 + jnp.exp(-x)))


def _gelu(x):
    return 0.5 * x * (1.0 + lax.erf(x * (1.0 / math.sqrt(2.0))))


def _mod_kernel(c_ref, w_ref, b_ref, o_ref):
    cs = _silu(c_ref[...]).astype(BF16)
    w = w_ref[...].astype(BF16)
    o_ref[...] = jnp.dot(cs, w, preferred_element_type=F32) + b_ref[...]


def _modulation(c8, w_ada, b_ada):
    d, n = w_ada.shape
    tn = 1024
    return pl.pallas_call(
        _mod_kernel,
        grid=(n // tn,),
        in_specs=[
            pl.BlockSpec((8, d), lambda j: (0, 0)),
            pl.BlockSpec((d, tn), lambda j: (0, j)),
            pl.BlockSpec((1, tn), lambda j: (0, j)),
        ],
        out_specs=pl.BlockSpec((8, tn), lambda j: (0, j)),
        out_shape=jax.ShapeDtypeStruct((8, n), F32),
        compiler_params=pltpu.CompilerParams(
            dimension_semantics=("arbitrary",), vmem_limit_bytes=VMEM_LIMIT),
        name="adaln_mod",
    )(c8, w_ada, b_ada)


def _rope_heads(t, cos, sin_signed, lane_lo):
    outs = []
    for hh in range(SEC // LANES):
        xh = t[:, hh * LANES:(hh + 1) * LANES]
        partner = jnp.where(lane_lo, pltpu.roll(xh, LANES - 32, 1), pltpu.roll(xh, 32, 1))
        outs.append(xh * cos + partner * sin_signed)
    return outs


def _proj_kernel(x_ref, pos_ref, scale_ref, shift_ref, ng_ref, w_ref, inv_ref,
                 lng_ref, lnb_ref, ws_ref, bs_ref,
                 ya_ref, q_ref, k_ref, vt_ref, zb_ref, vln_ref, sv_ref):
    tm = x_ref.shape[1]
    x = x_ref[0]
    ms = jnp.mean(x * x, axis=-1, keepdims=True)
    xn = (x * lax.rsqrt(ms + EPS)) * ng_ref[...]
    h = xn * (1.0 + scale_ref[0]) + shift_ref[0]
    hb = h.astype(BF16)

    def proj(sec):
        return jnp.dot(hb, w_ref[:, sec * SEC:(sec + 1) * SEC], preferred_element_type=F32)

    v = _gelu(proj(1))
    for g in range(GM_GROUPS):
        vg = v[:, g * LANES:(g + 1) * LANES]
        mu = jnp.mean(vg, axis=-1, keepdims=True)
        vc = vg - mu
        var = jnp.mean(vc * vc, axis=-1, keepdims=True)
        vn = vc * lax.rsqrt(var + EPS)
        vn = vn * lng_ref[:, g * LANES:(g + 1) * LANES] + lnb_ref[:, g * LANES:(g + 1) * LANES]
        vln_ref[:, g * LANES:(g + 1) * LANES] = vn.astype(BF16)

    row_c = lax.broadcasted_iota(jnp.int32, (GM_WINDOW, GM_WINDOW), 0) // CHUNK
    col_c = lax.broadcasted_iota(jnp.int32, (GM_WINDOW, GM_WINDOW), 1) // CHUNK
    tri = row_c >= col_c
    for g in range(GM_GROUPS):
        wsm = jnp.where(tri, ws_ref[g], 0.0).astype(BF16)
        bias = bs_ref[g]
        for w in range(tm // GM_WINDOW):
            rows = slice(w * GM_WINDOW, (w + 1) * GM_WINDOW)
            cols = slice(g * LANES, (g + 1) * LANES)
            sv_ref[rows, cols] = jnp.dot(wsm, vln_ref[rows, cols],
                                         preferred_element_type=F32) + bias

    t = _gelu(proj(0)) * sv_ref[...]
    ya_ref[0] = (t * _silu(proj(2))).astype(BF16)

    ang = pos_ref[0].astype(F32) * inv_ref[...]
    cos = jnp.cos(ang)
    sin = jnp.sin(ang)
    lane = lax.broadcasted_iota(jnp.int32, (1, LANES), 1)
    lane_lo = (lane % DIFF_HEAD_DIM) < (DIFF_HEAD_DIM // 2)
    sin_signed = jnp.where(lane_lo, -sin, sin)

    qscale = (DIFF_HEAD_DIM ** -0.5) * LOG2E
    for hh, qh in enumerate(_rope_heads(proj(3), cos, sin_signed, lane_lo)):
        q_ref[0, hh] = (qh * qscale).astype(BF16)
    for hh, kh in enumerate(_rope_heads(proj(4), cos, sin_signed, lane_lo)):
        k_ref[0, hh] = kh.astype(BF16)
    vd = proj(5)
    for hh in range(DIFF_HEADS):
        for sub in range(tm // TK_ATT):
            blk = vd[sub * TK_ATT:(sub + 1) * TK_ATT, hh * LANES:(hh + 1) * LANES]
            vt_ref[0, hh, sub] = blk.T.astype(BF16)
    zb_ref[0] = _silu(proj(6)).astype(BF16)


def _projection(x, pos3, scale, shift, norm_g, w_in, inv128, ln_g, ln_b, ws, bs3):
    b, s, d = x.shape
    tm = TM_PROJ
    head_shape = jax.ShapeDtypeStruct((b, DIFF_HEADS, s, LANES), BF16)
    row_shape = jax.ShapeDtypeStruct((b, s, SEC), BF16)
    head_spec = pl.BlockSpec((1, DIFF_HEADS, tm, LANES), lambda bi, i: (bi, 0, i, 0))
    row_spec = pl.BlockSpec((1, tm, SEC), lambda bi, i: (bi, i, 0))
    vt_shape = jax.ShapeDtypeStruct((b, DIFF_HEADS, s // TK_ATT, DIFF_V_DIM, TK_ATT), BF16)
    vt_spec = pl.BlockSpec((1, DIFF_HEADS, tm // TK_ATT, DIFF_V_DIM, TK_ATT),
                           lambda bi, i: (bi, 0, i, 0, 0))
    const2 = lambda bi, i: (0, 0)
    const3 = lambda bi, i: (0, 0, 0)
    return pl.pallas_call(
        _proj_kernel,
        grid=(b, s // tm),
        in_specs=[
            pl.BlockSpec((1, tm, d), lambda bi, i: (bi, i, 0)),
            pl.BlockSpec((1, tm, 1), lambda bi, i: (bi, i, 0)),
            pl.BlockSpec((1, 1, d), lambda bi, i: (bi, 0, 0)),
            pl.BlockSpec((1, 1, d), lambda bi, i: (bi, 0, 0)),
            pl.BlockSpec((1, d), const2),
            pl.BlockSpec((d, N_SEC * SEC), const2),
            pl.BlockSpec((1, LANES), const2),
            pl.BlockSpec((1, SEC), const2),
            pl.BlockSpec((1, SEC), const2),
            pl.BlockSpec((GM_GROUPS, GM_WINDOW, GM_WINDOW), const3),
            pl.BlockSpec((GM_GROUPS, GM_WINDOW, 1), const3),
        ],
        out_specs=[row_spec, head_spec, head_spec, vt_spec, row_spec],
        out_shape=[row_shape, head_shape, head_shape, vt_shape, row_shape],
        scratch_shapes=[pltpu.VMEM((tm, SEC), BF16), pltpu.VMEM((tm, SEC), F32)],
        compiler_params=pltpu.CompilerParams(
            dimension_semantics=("arbitrary", "arbitrary"), vmem_limit_bytes=VMEM_LIMIT),
        name="in_proj_gmlp_rope",
    )(x, pos3, scale, shift, norm_g, w_in, inv128, ln_g, ln_b, ws, bs3)


def _attn_kernel(lam_init, q_ref, k_ref, vt_ref, zb_ref, g_ref, lq1_ref, lk1_ref, lq2_ref,
                 lk2_ref, o_ref, q2_ref, acc_ref, m_ref, l_ref):
    s_len = q_ref.shape[2]
    tq, tk, ng = TQ_ATT, TK_ATT, G_ATT
    n_super = s_len // SUPER_ATT
    tiles_per_super = SUPER_ATT // tk

    lam = (jnp.exp(jnp.sum(lq1_ref[...] * lk1_ref[...], axis=-1, keepdims=True))
           - jnp.exp(jnp.sum(lq2_ref[...] * lk2_ref[...], axis=-1, keepdims=True))
           + lam_init)

    lane = lax.broadcasted_iota(jnp.int32, (1, LANES), 1)
    first_comp = lane < DIFF_HEAD_DIM
    key_c = lax.broadcasted_iota(jnp.int32, (tk, 2 * tq), 0) // CHUNK
    qry_c = lax.broadcasted_iota(jnp.int32, (tk, 2 * tq), 1) % tq // CHUNK
    diag_allowed = key_c <= qry_c

    def scores(job):
        g, kt, _, masked = job
        s = lax.dot_general(kt, q2_ref[g], (((1,), (1,)), ((), ())),
                            preferred_element_type=F32)
        if masked:
            s = jnp.where(diag_allowed, s, -jnp.inf)
        return s

    def softmax_update(job, s):
        g = job[0]
        m_prev = m_ref[g]
        m_new = jnp.maximum(m_prev, jnp.max(s, axis=0, keepdims=True))
        alpha = jnp.exp2(m_prev - m_new)
        p = jnp.exp2(s - m_new)
        l_ref[g] = alpha * l_ref[g] + jnp.sum(p, axis=0, keepdims=True)
        m_ref[g] = m_new
        return alpha, p.astype(BF16)

    def accumulate(job, alpha, p):
        g, _, vt, _ = job
        acc_ref[g] = alpha * acc_ref[g] + jnp.dot(vt, p, preferred_element_type=F32)

    def run_chains(jobs):
        n = len(jobs)
        s_vals, p_vals = {}, {}
        for step in range(n + 2):
            if step < n:
                s_vals[step] = scores(jobs[step])
            if 0 <= step - 1 < n:
                p_vals[step - 1] = softmax_update(jobs[step - 1], s_vals.pop(step - 1))
            if 0 <= step - 2 < n:
                accumulate(jobs[step - 2], *p_vals.pop(step - 2))

    def super_block(si, carry):
        base = pl.multiple_of(si * SUPER_ATT, SUPER_ATT)
        for g in range(ng):
            qb = q_ref[0, 0, pl.ds(base + g * tq, tq), :]
            zero = jnp.zeros_like(qb)
            q2_ref[g, :tq, :] = jnp.where(first_comp, qb, zero)
            q2_ref[g, tq:, :] = jnp.where(first_comp, zero, qb)
        m_ref[...] = jnp.full(m_ref.shape, -jnp.inf, F32)
        l_ref[...] = jnp.zeros(l_ref.shape, F32)
        acc_ref[...] = jnp.zeros(acc_ref.shape, F32)

        def full_tiles(jj, c):
            jobs = []
            for u in range(FULL_TILES_PER_ITER):
                j = jj * FULL_TILES_PER_ITER + u
                kt = k_ref[0, 0, pl.ds(pl.multiple_of(j * tk, tk), tk), :]
                vt = vt_ref[0, 0, j]
                jobs += [(g, kt, vt, False) for g in range(ng)]
            run_chains(jobs)
            return c

        lax.fori_loop(0, si * (tiles_per_super // FULL_TILES_PER_ITER), full_tiles, 0)

        jobs = []
        for bt in range(tiles_per_super):
            j = si * tiles_per_super + bt
            kt = k_ref[0, 0, pl.ds(pl.multiple_of(j * tk, tk), tk), :]
            vt = vt_ref[0, 0, j]
            jobs += [(g, kt, vt, g == bt) for g in range(bt, ng)]
        run_chains(jobs)

        for g in range(ng):
            o_all = acc_ref[g] / l_ref[g]
            o = o_all[:, :tq] - lam * o_all[:, tq:]
            ms = jnp.mean(o * o, axis=0, keepdims=True)
            on = (o * lax.rsqrt(ms + EPS)) * g_ref[...]
            on = (on * (1.0 - lam_init)).T
            rows = pl.ds(base + g * tq, tq)
            o_ref[0, rows, :] = (on * zb_ref[0, rows, :].astype(F32)).astype(BF16)
        return carry

    lax.fori_loop(0, n_super, super_block, 0)


def _attention(q, k, vt, zb, subln_g_col, lq1, lk1, lq2, lk2, lam_init):
    b, h, s, _ = q.shape
    head_spec = pl.BlockSpec((1, 1, s, LANES), lambda bi, hi: (bi, hi, 0, 0))
    vt_spec = pl.BlockSpec((1, 1, s // TK_ATT, DIFF_V_DIM, TK_ATT),
                           lambda bi, hi: (bi, hi, 0, 0, 0))
    col_spec = pl.BlockSpec((1, s, LANES), lambda bi, hi: (bi, 0, hi))
    vec64 = pl.BlockSpec((1, DIFF_HEAD_DIM), lambda bi, hi: (0, 0))
    return pl.pallas_call(
        functools.partial(_attn_kernel, lam_init),
        grid=(b, h),
        in_specs=[head_spec, head_spec, vt_spec, col_spec,
                  pl.BlockSpec((DIFF_V_DIM, 1), lambda bi, hi: (0, 0)),
                  vec64, vec64, vec64, vec64],
        out_specs=col_spec,
        out_shape=jax.ShapeDtypeStruct((b, s, h * LANES), BF16),
        scratch_shapes=[pltpu.VMEM((G_ATT, 2 * TQ_ATT, LANES), BF16),
                        pltpu.VMEM((G_ATT, DIFF_V_DIM, 2 * TQ_ATT), F32),
                        pltpu.VMEM((G_ATT, 1, 2 * TQ_ATT), F32),
                        pltpu.VMEM((G_ATT, 1, 2 * TQ_ATT), F32)],
        compiler_params=pltpu.CompilerParams(
            dimension_semantics=("arbitrary", "arbitrary"), vmem_limit_bytes=VMEM_LIMIT),
        name="diff_attention",
    )(q, k, vt, zb, subln_g_col, lq1, lk1, lq2, lk2)


def _out_kernel(ya_ref, yb_ref, x_ref, gate_ref, w_ref, fg_ref, o_ref):
    y = jnp.dot(ya_ref[0], w_ref[:SEC, :], preferred_element_type=F32)
    y = y + jnp.dot(yb_ref[0], w_ref[SEC:, :], preferred_element_type=F32)
    xo = x_ref[0] + gate_ref[0] * y
    ms = jnp.mean(xo * xo, axis=-1, keepdims=True)
    o_ref[0] = (xo * lax.rsqrt(ms + EPS)) * fg_ref[...]


def _out_projection(ya, yb, x, gate, w_out, final_g):
    b, s, d = x.shape
    tm = TM_PROJ
    return pl.pallas_call(
        _out_kernel,
        grid=(b, s // tm),
        in_specs=[
            pl.BlockSpec((1, tm, SEC), lambda bi, i: (bi, i, 0)),
            pl.BlockSpec((1, tm, SEC), lambda bi, i: (bi, i, 0)),
            pl.BlockSpec((1, tm, d), lambda bi, i: (bi, i, 0)),
            pl.BlockSpec((1, 1, d), lambda bi, i: (bi, 0, 0)),
            pl.BlockSpec((2 * SEC, d), lambda bi, i: (0, 0)),
            pl.BlockSpec((1, d), lambda bi, i: (0, 0)),
        ],
        out_specs=pl.BlockSpec((1, tm, d), lambda bi, i: (bi, i, 0)),
        out_shape=jax.ShapeDtypeStruct((b, s, d), F32),
        compiler_params=pltpu.CompilerParams(
            dimension_semantics=("arbitrary", "arbitrary"), vmem_limit_bytes=VMEM_LIMIT),
        name="out_proj_residual_norm",
    )(ya, yb, x, gate, w_out, final_g)


def kernel(x, c, positions, w_ada, b_ada, norm_g, w_in, gm_ln_g, gm_ln_b, gm_ws, gm_bs,
           lam_q1, lam_k1, lam_q2, lam_k2, diff_subln_g, w_out, final_g):
    b, s, d = x.shape
    depth = w_ada.shape[0]
    half = DIFF_HEAD_DIM // 2
    inv = ROPE_THETA ** (-jnp.arange(half, dtype=F32) * 2.0 / DIFF_HEAD_DIM)
    inv128 = jnp.tile(inv, LANES // half)[None, :]
    pos3 = positions.reshape(b, s, 1)
    c8 = jnp.zeros((8, d), F32).at[:b].set(c)
    assert depth == 1
    l = 0
    mod = _modulation(c8, w_ada[l], b_ada[l][None, :])[:b]
    shift = mod[:, None, :d]
    scale = mod[:, None, d:2 * d]
    gate = mod[:, None, 2 * d:]
    ya, q, k, vt, zb = _projection(
        x, pos3, scale, shift, norm_g[l][None, :], w_in[l].astype(BF16), inv128,
        gm_ln_g[l][None, :], gm_ln_b[l][None, :], gm_ws[l], gm_bs[l][:, :, None])
    lam_init = 0.8 - 0.6 * math.exp(-0.3 * l)
    yb = _attention(q, k, vt, zb, diff_subln_g[l][:, None], lam_q1[l][None, :],
                    lam_k1[l][None, :], lam_q2[l][None, :], lam_k2[l][None, :], lam_init)
    return _out_projection(ya, yb, x, gate, w_out[l].astype(BF16), final_g[None, :])
```

```python
import functools
import math

import jax
import jax.numpy as jnp
from jax import lax
from jax.experimental import pallas as pl
from jax.experimental.pallas import tpu as pltpu

F32 = jnp.float32
BF16 = jnp.bfloat16

EPS = 1e-6
CHUNK = 64
GM_GROUPS = 4
GM_GROUP_DIM = 128
GM_WINDOW = 128
DIFF_HEADS = 4
DIFF_HEAD_DIM = 64
DIFF_V_DIM = 128
ROPE_THETA = 10000.0
SEC = 512
N_SEC = 7
LOG2E = 1.4426950408889634

LANES = 128
VMEM_LIMIT = 48 * 1024 * 1024

TM_PROJ = 512
TQ_ATT = 256
TK_ATT = 256
G_ATT = 4
SUPER_ATT = TQ_ATT * G_ATT
FULL_TILES_PER_ITER = 4
ONES_ROWS = 16
assert SUPER_ATT % TK_ATT == 0 and TQ_ATT == TK_ATT and TK_ATT % CHUNK == 0
assert (SUPER_ATT // TK_ATT) % FULL_TILES_PER_ITER == 0


def _silu(x):
    return x * (1.0 / (1.0 + jnp.exp(-x)))


def _gelu(x):
    return 0.5 * x * (1.0 + lax.erf(x * (1.0 / math.sqrt(2.0))))


def _mod_kernel(c_ref, w_ref, b_ref, o_ref):
    cs = _silu(c_ref[...]).astype(BF16)
    w = w_ref[...].astype(BF16)
    o_ref[...] = jnp.dot(cs, w, preferred_element_type=F32) + b_ref[...]


def _modulation(c8, w_ada, b_ada):
    d, n = w_ada.shape
    tn = 1024
    return pl.pallas_call(
        _mod_kernel,
        grid=(n // tn,),
        in_specs=[
            pl.BlockSpec((8, d), lambda j: (0, 0)),
            pl.BlockSpec((d, tn), lambda j: (0, j)),
            pl.BlockSpec((1, tn), lambda j: (0, j)),
        ],
        out_specs=pl.BlockSpec((8, tn), lambda j: (0, j)),
        out_shape=jax.ShapeDtypeStruct((8, n), F32),
        compiler_params=pltpu.CompilerParams(
            dimension_semantics=("arbitrary",), vmem_limit_bytes=VMEM_LIMIT),
        name="adaln_mod",
    )(c8, w_ada, b_ada)


def _rope_heads(t, cos, sin_signed, lane_lo):
    outs = []
    for hh in range(SEC // LANES):
        xh = t[:, hh * LANES:(hh + 1) * LANES]
        partner = jnp.where(lane_lo, pltpu.roll(xh, LANES - 32, 1), pltpu.roll(xh, 32, 1))
        outs.append(xh * cos + partner * sin_signed)
    return outs


def _proj_kernel(x_ref, pos_ref, scale_ref, shift_ref, ng_ref, w_ref, inv_ref,
                 lng_ref, lnb_ref, ws_ref, bs_ref,
                 ya_ref, q_ref, k_ref, vt_ref, zb_ref, vln_ref, sv_ref):
    tm = x_ref.shape[1]

    pos = pos_ref[0, 0].astype(F32)
    angs = []
    for a in range(tm // LANES):
        col = jnp.broadcast_to(pos[a:a + 1, :], (LANES, LANES)).T
        angs.append(col * inv_ref[...])
    ang = jnp.concatenate(angs, axis=0)
    cos = jnp.cos(ang)
    sin = jnp.sin(ang)
    lane = lax.broadcasted_iota(jnp.int32, (1, LANES), 1)
    lane_lo = (lane % DIFF_HEAD_DIM) < (DIFF_HEAD_DIM // 2)
    sin_signed = jnp.where(lane_lo, -sin, sin)

    x = x_ref[0]
    ms = jnp.mean(x * x, axis=-1, keepdims=True)
    xn = (x * lax.rsqrt(ms + EPS)) * ng_ref[...]
    h = xn * (1.0 + scale_ref[0]) + shift_ref[0]
    hb = h.astype(BF16)

    def proj(sec):
        return jnp.dot(hb, w_ref[:, sec * SEC:(sec + 1) * SEC], preferred_element_type=F32)

    pv = proj(1)
    pu = proj(0)
    v = _gelu(pv)
    for g in range(GM_GROUPS):
        vg = v[:, g * LANES:(g + 1) * LANES]
        mu = jnp.mean(vg, axis=-1, keepdims=True)
        vc = vg - mu
        var = jnp.mean(vc * vc, axis=-1, keepdims=True)
        vn = vc * lax.rsqrt(var + EPS)
        vn = vn * lng_ref[:, g * LANES:(g + 1) * LANES] + lnb_ref[:, g * LANES:(g + 1) * LANES]
        vln_ref[:, g * LANES:(g + 1) * LANES] = vn.astype(BF16)
    pz = proj(2)
    gu = _gelu(pu)

    row_c = lax.broadcasted_iota(jnp.int32, (GM_WINDOW, GM_WINDOW), 0) // CHUNK
    col_c = lax.broadcasted_iota(jnp.int32, (GM_WINDOW, GM_WINDOW), 1) // CHUNK
    tri = row_c >= col_c
    n_win = tm // GM_WINDOW
    for g in range(GM_GROUPS):
        wsm = jnp.where(tri, ws_ref[g], 0.0).astype(BF16)
        cols = slice(g * LANES, (g + 1) * LANES)
        vwin = jnp.concatenate(
            [vln_ref[w * GM_WINDOW:(w + 1) * GM_WINDOW, cols] for w in range(n_win)], axis=1)
        svg = jnp.dot(wsm, vwin, preferred_element_type=F32) + bs_ref[g]
        for w in range(n_win):
            sv_ref[w * GM_WINDOW:(w + 1) * GM_WINDOW, cols] = svg[:, w * LANES:(w + 1) * LANES]

    pq = proj(3)
    ya_ref[0] = (gu * sv_ref[...] * _silu(pz)).astype(BF16)
    pk = proj(4)
    qscale = (DIFF_HEAD_DIM ** -0.5) * LOG2E
    for hh, qh in enumerate(_rope_heads(pq, cos, sin_signed, lane_lo)):
        q_ref[0, hh] = (qh * qscale).astype(BF16)
    pvd = proj(5)
    for hh, kh in enumerate(_rope_heads(pk, cos, sin_signed, lane_lo)):
        k_ref[0, hh] = kh.astype(BF16)
    pzb = proj(6)
    for hh in range(DIFF_HEADS):
        for sub in range(tm // TK_ATT):
            blk = pvd[sub * TK_ATT:(sub + 1) * TK_ATT, hh * LANES:(hh + 1) * LANES]
            vt_ref[0, hh, sub] = blk.T.astype(BF16)
    zb_ref[0] = _silu(pzb).astype(BF16)


def _projection(x, pos4, scale, shift, norm_g, w_in, inv128, ln_g, ln_b, ws, bs3):
    b, s, d = x.shape
    tm = TM_PROJ
    head_shape = jax.ShapeDtypeStruct((b, DIFF_HEADS, s, LANES), BF16)
    row_shape = jax.ShapeDtypeStruct((b, s, SEC), BF16)
    head_spec = pl.BlockSpec((1, DIFF_HEADS, tm, LANES), lambda bi, i: (bi, 0, i, 0))
    row_spec = pl.BlockSpec((1, tm, SEC), lambda bi, i: (bi, i, 0))
    vt_shape = jax.ShapeDtypeStruct((b, DIFF_HEADS, s // TK_ATT, DIFF_V_DIM, TK_ATT), BF16)
    vt_spec = pl.BlockSpec((1, DIFF_HEADS, tm // TK_ATT, DIFF_V_DIM, TK_ATT),
                           lambda bi, i: (bi, 0, i, 0, 0))
    const2 = lambda bi, i: (0, 0)
    const3 = lambda bi, i: (0, 0, 0)
    return pl.pallas_call(
        _proj_kernel,
        grid=(b, s // tm),
        in_specs=[
            pl.BlockSpec((1, tm, d), lambda bi, i: (bi, i, 0)),
            pl.BlockSpec((1, 1, tm // LANES, LANES), lambda bi, i: (bi, i, 0, 0)),
            pl.BlockSpec((1, 1, d), lambda bi, i: (bi, 0, 0)),
            pl.BlockSpec((1, 1, d), lambda bi, i: (bi, 0, 0)),
            pl.BlockSpec((1, d), const2),
            pl.BlockSpec((d, N_SEC * SEC), const2),
            pl.BlockSpec((1, LANES), const2),
            pl.BlockSpec((1, SEC), const2),
            pl.BlockSpec((1, SEC), const2),
            pl.BlockSpec((GM_GROUPS, GM_WINDOW, GM_WINDOW), const3),
            pl.BlockSpec((GM_GROUPS, GM_WINDOW, 1), const3),
        ],
        out_specs=[row_spec, head_spec, head_spec, vt_spec, row_spec],
        out_shape=[row_shape, head_shape, head_shape, vt_shape, row_shape],
        scratch_shapes=[pltpu.VMEM((tm, SEC), BF16), pltpu.VMEM((tm, SEC), F32)],
        compiler_params=pltpu.CompilerParams(
            dimension_semantics=("arbitrary", "arbitrary"), vmem_limit_bytes=VMEM_LIMIT),
        name="in_proj_gmlp_rope",
    )(x, pos4, scale, shift, norm_g, w_in, inv128, ln_g, ln_b, ws, bs3)


def _attn_kernel(lam_init, q_ref, k_ref, vt_ref, zb_ref, g_ref, lq1_ref, lk1_ref, lq2_ref,
                 lk2_ref, o_ref, q2_ref, acc_ref, m_ref):
    s_len = q_ref.shape[2]
    tq, tk, ng = TQ_ATT, TK_ATT, G_ATT
    n_super = s_len // SUPER_ATT
    tiles_per_super = SUPER_ATT // tk

    lam = (jnp.exp(jnp.sum(lq1_ref[...] * lk1_ref[...], axis=-1, keepdims=True))
           - jnp.exp(jnp.sum(lq2_ref[...] * lk2_ref[...], axis=-1, keepdims=True))
           + lam_init)

    lane = lax.broadcasted_iota(jnp.int32, (1, LANES), 1)
    first_comp = lane < DIFF_HEAD_DIM
    key_c = lax.broadcasted_iota(jnp.int32, (tk, 2 * tq), 0) // CHUNK
    qry_c = lax.broadcasted_iota(jnp.int32, (tk, 2 * tq), 1) % tq // CHUNK
    diag_allowed = key_c <= qry_c
    ones_rows = jnp.ones((ONES_ROWS, tk), BF16)

    def scores(job):
        g, j, _ = job
        kt = k_ref[0, 0, pl.ds(pl.multiple_of(j * tk, tk), tk), :]
        return lax.dot_general(kt, q2_ref[g], (((1,), (1,)), ((), ())),
                               preferred_element_type=F32)

    def softmax_update(job, s):
        g, _, masked = job
        if masked:
            s = jnp.where(diag_allowed, s, -jnp.inf)
        m_prev = m_ref[g]
        m_new = jnp.maximum(m_prev, jnp.max(s, axis=0, keepdims=True))
        m_ref[g] = m_new
        return jnp.exp2(m_prev - m_new), jnp.exp2(s - m_new).astype(BF16)

    def accumulate(job, alpha, p):
        g, j, _ = job
        vte = jnp.concatenate([vt_ref[0, 0, j], ones_rows], axis=0)
        acc_ref[g] = alpha * acc_ref[g] + jnp.dot(vte, p, preferred_element_type=F32)

    def run_stream(jobs, lookahead, pending_job, state):
        s_cur, alpha, p = state
        pend = (pending_job, alpha, p)
        for i, job in enumerate(jobs):
            nxt = jobs[i + 1] if i + 1 < len(jobs) else lookahead
            s_nxt = scores(nxt) if nxt is not None else None
            alpha, p = softmax_update(job, s_cur)
            accumulate(*pend)
            pend = (job, alpha, p)
            s_cur = s_nxt
        if lookahead is None:
            accumulate(*pend)
            return None
        return s_cur, pend[1], pend[2]

    def start_super_block(si):
        base = pl.multiple_of(si * SUPER_ATT, SUPER_ATT)
        for g in range(ng):
            qb = q_ref[0, 0, pl.ds(base + g * tq, tq), :]
            zero = jnp.zeros_like(qb)
            q2_ref[g, :tq, :] = jnp.where(first_comp, qb, zero)
            q2_ref[g, tq:, :] = jnp.where(first_comp, zero, qb)
        m_ref[...] = jnp.full(m_ref.shape, -jnp.inf, F32)
        acc_ref[...] = jnp.zeros(acc_ref.shape, F32)
        return scores((0, 0, False))

    def super_block(si, s_first):
        base = pl.multiple_of(si * SUPER_ATT, SUPER_ATT)

        def full_tiles(jj, state):
            j0 = jj * FULL_TILES_PER_ITER
            jobs = [(g, j0 + u, False) for u in range(FULL_TILES_PER_ITER) for g in range(ng)]
            return run_stream(jobs, (0, j0 + FULL_TILES_PER_ITER, False),
                              (ng - 1, jnp.maximum(j0 - 1, 0), False), state)

        state = (s_first, jnp.ones((1, 2 * tq), F32), jnp.zeros((tk, 2 * tq), BF16))
        n_full = si * (tiles_per_super // FULL_TILES_PER_ITER)
        state = lax.fori_loop(0, n_full, full_tiles, state)

        j0 = si * tiles_per_super
        jobs = [(g, j0 + bt, g == bt) for bt in range(tiles_per_super) for g in range(bt, ng)]
        run_stream(jobs, None, (ng - 1, jnp.maximum(j0 - 1, 0), False), state)

        outs = []
        for g in range(ng):
            acc = acc_ref[g]
            o_all = acc[:DIFF_V_DIM] / acc[DIFF_V_DIM:DIFF_V_DIM + 1]
            o = o_all[:, :tq] - lam * o_all[:, tq:]
            ms = jnp.mean(o * o, axis=0, keepdims=True)
            on = (o * lax.rsqrt(ms + EPS)) * g_ref[...]
            outs.append((on * (1.0 - lam_init)).T)
        s_next = start_super_block(jnp.minimum(si + 1, n_super - 1))
        for g in range(ng):
            rows = pl.ds(base + g * tq, tq)
            o_ref[0, rows, :] = (outs[g] * zb_ref[0, rows, :].astype(F32)).astype(BF16)
        return s_next

    lax.fori_loop(0, n_super, super_block, start_super_block(0))


def _attention(q, k, vt, zb, subln_g_col, lq1, lk1, lq2, lk2, lam_init):
    b, h, s, _ = q.shape
    head_spec = pl.BlockSpec((1, 1, s, LANES), lambda bi, hi: (bi, hi, 0, 0))
    vt_spec = pl.BlockSpec((1, 1, s // TK_ATT, DIFF_V_DIM, TK_ATT),
                           lambda bi, hi: (bi, hi, 0, 0, 0))
    col_spec = pl.BlockSpec((1, s, LANES), lambda bi, hi: (bi, 0, hi))
    vec64 = pl.BlockSpec((1, DIFF_HEAD_DIM), lambda bi, hi: (0, 0))
    return pl.pallas_call(
        functools.partial(_attn_kernel, lam_init),
        grid=(b, h),
        in_specs=[head_spec, head_spec, vt_spec, col_spec,
                  pl.BlockSpec((DIFF_V_DIM, 1), lambda bi, hi: (0, 0)),
                  vec64, vec64, vec64, vec64],
        out_specs=col_spec,
        out_shape=jax.ShapeDtypeStruct((b, s, h * LANES), BF16),
        scratch_shapes=[pltpu.VMEM((G_ATT, 2 * TQ_ATT, LANES), BF16),
                        pltpu.VMEM((G_ATT, DIFF_V_DIM + ONES_ROWS, 2 * TQ_ATT), F32),
                        pltpu.VMEM((G_ATT, 1, 2 * TQ_ATT), F32)],
        compiler_params=pltpu.CompilerParams(
            dimension_semantics=("arbitrary", "arbitrary"), vmem_limit_bytes=VMEM_LIMIT),
        name="diff_attention",
    )(q, k, vt, zb, subln_g_col, lq1, lk1, lq2, lk2)


def _out_kernel(ya_ref, yb_ref, x_ref, gate_ref, w_ref, fg_ref, o_ref):
    y = jnp.dot(ya_ref[0], w_ref[:SEC, :], preferred_element_type=F32)
    y = y + jnp.dot(yb_ref[0], w_ref[SEC:, :], preferred_element_type=F32)
    xo = x_ref[0] + gate_ref[0] * y
    ms = jnp.mean(xo * xo, axis=-1, keepdims=True)
    o_ref[0] = (xo * lax.rsqrt(ms + EPS)) * fg_ref[...]


def _out_projection(ya, yb, x, gate, w_out, final_g):
    b, s, d = x.shape
    tm = TM_PROJ
    return pl.pallas_call(
        _out_kernel,
        grid=(b, s // tm),
        in_specs=[
            pl.BlockSpec((1, tm, SEC), lambda bi, i: (bi, i, 0)),
            pl.BlockSpec((1, tm, SEC), lambda bi, i: (bi, i, 0)),
            pl.BlockSpec((1, tm, d), lambda bi, i: (bi, i, 0)),
            pl.BlockSpec((1, 1, d), lambda bi, i: (bi, 0, 0)),
            pl.BlockSpec((2 * SEC, d), lambda bi, i: (0, 0)),
            pl.BlockSpec((1, d), lambda bi, i: (0, 0)),
        ],
        out_specs=pl.BlockSpec((1, tm, d), lambda bi, i: (bi, i, 0)),
        out_shape=jax.ShapeDtypeStruct((b, s, d), F32),
        compiler_params=pltpu.CompilerParams(
            dimension_semantics=("arbitrary", "arbitrary"), vmem_limit_bytes=VMEM_LIMIT),
        name="out_proj_residual_norm",
    )(ya, yb, x, gate, w_out, final_g)


def kernel(x, c, positions, w_ada, b_ada, norm_g, w_in, gm_ln_g, gm_ln_b, gm_ws, gm_bs,
           lam_q1, lam_k1, lam_q2, lam_k2, diff_subln_g, w_out, final_g):
    b, s, d = x.shape
    depth = w_ada.shape[0]
    half = DIFF_HEAD_DIM // 2
    inv = ROPE_THETA ** (-jnp.arange(half, dtype=F32) * 2.0 / DIFF_HEAD_DIM)
    inv128 = jnp.tile(inv, LANES // half)[None, :]
    pos4 = positions.reshape(b, s // TM_PROJ, TM_PROJ // LANES, LANES)
    c8 = jnp.zeros((8, d), F32).at[:b].set(c)
    assert depth == 1
    l = 0
    mod = _modulation(c8, w_ada[l], b_ada[l][None, :])[:b]
    shift = mod[:, None, :d]
    scale = mod[:, None, d:2 * d]
    gate = mod[:, None, 2 * d:]
    ya, q, k, vt, zb = _projection(
        x, pos4, scale, shift, norm_g[l][None, :], w_in[l].astype(BF16), inv128,
        gm_ln_g[l][None, :], gm_ln_b[l][None, :], gm_ws[l], gm_bs[l][:, :, None])
    lam_init = 0.8 - 0.6 * math.exp(-0.3 * l)
    yb = _attention(q, k, vt, zb, diff_subln_g[l][:, None], lam_q1[l][None, :],
                    lam_k1[l][None, :], lam_q2[l][None, :], lam_k2[l][None, :], lam_init)
    return _out_projection(ya, yb, x, gate, w_out[l].astype(BF16), final_g[None, :])
```

```python
import functools
import math

import jax
import jax.numpy as jnp
from jax import lax
from jax.experimental import pallas as pl
from jax.experimental.pallas import tpu as pltpu

F32 = jnp.float32
BF16 = jnp.bfloat16

EPS = 1e-6
CHUNK = 64
GM_GROUPS = 4
GM_GROUP_DIM = 128
GM_WINDOW = 128
DIFF_HEADS = 4
DIFF_HEAD_DIM = 64
DIFF_V_DIM = 128
ROPE_THETA = 10000.0
SEC = 512
N_SEC = 7
LOG2E = 1.4426950408889634

LANES = 128
VMEM_LIMIT = 48 * 1024 * 1024

TM_PROJ = 512
WEIGHT_CAST_ROWS = 128
TQ_ATT = 256
TK_ATT = 256
G_ATT = 4
SUPER_ATT = TQ_ATT * G_ATT
FULL_TILES_PER_ITER = 4
ONES_ROWS = 16
assert SUPER_ATT % TK_ATT == 0 and TQ_ATT == TK_ATT and TK_ATT % CHUNK == 0
assert (SUPER_ATT // TK_ATT) % FULL_TILES_PER_ITER == 0


def _silu(x):
    return x * (1.0 / (1.0 + jnp.exp(-x)))


def _gelu(x):
    return 0.5 * x * (1.0 + lax.erf(x * (1.0 / math.sqrt(2.0))))


def _mod_kernel(c_ref, w_ref, b_ref, o_ref):
    cs = _silu(c_ref[...]).astype(BF16)
    w = w_ref[...].astype(BF16)
    o_ref[...] = jnp.dot(cs, w, preferred_element_type=F32) + b_ref[...]


def _modulation(c8, w_ada, b_ada):
    d, n = w_ada.shape
    tn = 1024
    return pl.pallas_call(
        _mod_kernel,
        grid=(n // tn,),
        in_specs=[
            pl.BlockSpec((8, d), lambda j: (0, 0)),
            pl.BlockSpec((d, tn), lambda j: (0, j)),
            pl.BlockSpec((1, tn), lambda j: (0, j)),
        ],
        out_specs=pl.BlockSpec((8, tn), lambda j: (0, j)),
        out_shape=jax.ShapeDtypeStruct((8, n), F32),
        compiler_params=pltpu.CompilerParams(
            dimension_semantics=("arbitrary",), vmem_limit_bytes=VMEM_LIMIT),
        name="adaln_mod",
    )(c8, w_ada, b_ada)


def _cast_weight_once(w_ref, wb_ref):
    first = (pl.program_id(0) == 0) & (pl.program_id(1) == 0)

    @pl.when(first)
    def _():
        def body(i, c):
            rows = pl.ds(pl.multiple_of(i * WEIGHT_CAST_ROWS, WEIGHT_CAST_ROWS), WEIGHT_CAST_ROWS)
            wb_ref[rows, :] = w_ref[rows, :].astype(BF16)
            return c
        lax.fori_loop(0, w_ref.shape[0] // WEIGHT_CAST_ROWS, body, 0)


def _rope_heads(t, cos, sin_signed, lane_lo):
    outs = []
    for hh in range(SEC // LANES):
        xh = t[:, hh * LANES:(hh + 1) * LANES]
        partner = jnp.where(lane_lo, pltpu.roll(xh, LANES - 32, 1), pltpu.roll(xh, 32, 1))
        outs.append(xh * cos + partner * sin_signed)
    return outs


def _proj_kernel(x_ref, pos_ref, scale_ref, shift_ref, ng_ref, w_ref, inv_ref,
                 lng_ref, lnb_ref, ws_ref, bs_ref,
                 ya_ref, q_ref, k_ref, vt_ref, zb_ref, vln_ref, sv_ref, wb_ref):
    tm = x_ref.shape[1]
    _cast_weight_once(w_ref, wb_ref)

    pos = pos_ref[0, 0].astype(F32)
    angs = []
    for a in range(tm // LANES):
        col = jnp.broadcast_to(pos[a:a + 1, :], (LANES, LANES)).T
        angs.append(col * inv_ref[...])
    ang = jnp.concatenate(angs, axis=0)
    cos = jnp.cos(ang)
    sin = jnp.sin(ang)
    lane = lax.broadcasted_iota(jnp.int32, (1, LANES), 1)
    lane_lo = (lane % DIFF_HEAD_DIM) < (DIFF_HEAD_DIM // 2)
    sin_signed = jnp.where(lane_lo, -sin, sin)

    x = x_ref[0]
    ms = jnp.mean(x * x, axis=-1, keepdims=True)
    xn = (x * lax.rsqrt(ms + EPS)) * ng_ref[...]
    h = xn * (1.0 + scale_ref[0]) + shift_ref[0]
    hb = h.astype(BF16)

    def proj(sec):
        return jnp.dot(hb, wb_ref[:, sec * SEC:(sec + 1) * SEC], preferred_element_type=F32)

    pv = proj(1)
    pu = proj(0)
    v = _gelu(pv)
    for g in range(GM_GROUPS):
        vg = v[:, g * LANES:(g + 1) * LANES]
        mu = jnp.mean(vg, axis=-1, keepdims=True)
        vc = vg - mu
        var = jnp.mean(vc * vc, axis=-1, keepdims=True)
        vn = vc * lax.rsqrt(var + EPS)
        vn = vn * lng_ref[:, g * LANES:(g + 1) * LANES] + lnb_ref[:, g * LANES:(g + 1) * LANES]
        vln_ref[:, g * LANES:(g + 1) * LANES] = vn.astype(BF16)
    pz = proj(2)
    gu = _gelu(pu)

    row_c = lax.broadcasted_iota(jnp.int32, (GM_WINDOW, GM_WINDOW), 0) // CHUNK
    col_c = lax.broadcasted_iota(jnp.int32, (GM_WINDOW, GM_WINDOW), 1) // CHUNK
    tri = row_c >= col_c
    n_win = tm // GM_WINDOW
    for g in range(GM_GROUPS):
        wsm = jnp.where(tri, ws_ref[g], 0.0).astype(BF16)
        cols = slice(g * LANES, (g + 1) * LANES)
        vwin = jnp.concatenate(
            [vln_ref[w * GM_WINDOW:(w + 1) * GM_WINDOW, cols] for w in range(n_win)], axis=1)
        svg = jnp.dot(wsm, vwin, preferred_element_type=F32) + bs_ref[g]
        for w in range(n_win):
            sv_ref[w * GM_WINDOW:(w + 1) * GM_WINDOW, cols] = svg[:, w * LANES:(w + 1) * LANES]

    pq = proj(3)
    ya_ref[0] = (gu * sv_ref[...] * _silu(pz)).astype(BF16)
    pk = proj(4)
    qscale = (DIFF_HEAD_DIM ** -0.5) * LOG2E
    for hh, qh in enumerate(_rope_heads(pq, cos, sin_signed, lane_lo)):
        q_ref[0, hh] = (qh * qscale).astype(BF16)
    pvd = proj(5)
    for hh, kh in enumerate(_rope_heads(pk, cos, sin_signed, lane_lo)):
        k_ref[0, hh] = kh.astype(BF16)
    pzb = proj(6)
    for hh in range(DIFF_HEADS):
        for sub in range(tm // TK_ATT):
            blk = pvd[sub * TK_ATT:(sub + 1) * TK_ATT, hh * LANES:(hh + 1) * LANES]
            vt_ref[0, hh, sub] = blk.T.astype(BF16)
    zb_ref[0] = _silu(pzb).astype(BF16)


def _projection(x, pos4, scale, shift, norm_g, w_in, inv128, ln_g, ln_b, ws, bs3):
    b, s, d = x.shape
    tm = TM_PROJ
    head_shape = jax.ShapeDtypeStruct((b, DIFF_HEADS, s, LANES), BF16)
    row_shape = jax.ShapeDtypeStruct((b, s, SEC), BF16)
    head_spec = pl.BlockSpec((1, DIFF_HEADS, tm, LANES), lambda bi, i: (bi, 0, i, 0))
    row_spec = pl.BlockSpec((1, tm, SEC), lambda bi, i: (bi, i, 0))
    vt_shape = jax.ShapeDtypeStruct((b, DIFF_HEADS, s // TK_ATT, DIFF_V_DIM, TK_ATT), BF16)
    vt_spec = pl.BlockSpec((1, DIFF_HEADS, tm // TK_ATT, DIFF_V_DIM, TK_ATT),
                           lambda bi, i: (bi, 0, i, 0, 0))
    const2 = lambda bi, i: (0, 0)
    const3 = lambda bi, i: (0, 0, 0)
    return pl.pallas_call(
        _proj_kernel,
        grid=(b, s // tm),
        in_specs=[
            pl.BlockSpec((1, tm, d), lambda bi, i: (bi, i, 0)),
            pl.BlockSpec((1, 1, tm // LANES, LANES), lambda bi, i: (bi, i, 0, 0)),
            pl.BlockSpec((1, 1, d), lambda bi, i: (bi, 0, 0)),
            pl.BlockSpec((1, 1, d), lambda bi, i: (bi, 0, 0)),
            pl.BlockSpec((1, d), const2),
            pl.BlockSpec((d, N_SEC * SEC), const2,
                         pipeline_mode=pl.Buffered(1)),
            pl.BlockSpec((1, LANES), const2),
            pl.BlockSpec((1, SEC), const2),
            pl.BlockSpec((1, SEC), const2),
            pl.BlockSpec((GM_GROUPS, GM_WINDOW, GM_WINDOW), const3),
            pl.BlockSpec((GM_GROUPS, GM_WINDOW, 1), const3),
        ],
        out_specs=[row_spec, head_spec, head_spec, vt_spec, row_spec],
        out_shape=[row_shape, head_shape, head_shape, vt_shape, row_shape],
        scratch_shapes=[pltpu.VMEM((tm, SEC), BF16), pltpu.VMEM((tm, SEC), F32),
                        pltpu.VMEM((d, N_SEC * SEC), BF16)],
        compiler_params=pltpu.CompilerParams(
            dimension_semantics=("arbitrary", "arbitrary"), vmem_limit_bytes=VMEM_LIMIT),
        name="in_proj_gmlp_rope",
    )(x, pos4, scale, shift, norm_g, w_in, inv128, ln_g, ln_b, ws, bs3)


def _attn_kernel(lam_init, q_ref, k_ref, vt_ref, zb_ref, g_ref, lq1_ref, lk1_ref, lq2_ref,
                 lk2_ref, o_ref, q2_ref, acc_ref, m_ref):
    s_len = q_ref.shape[2]
    tq, tk, ng = TQ_ATT, TK_ATT, G_ATT
    n_super = s_len // SUPER_ATT
    tiles_per_super = SUPER_ATT // tk

    lam = (jnp.exp(jnp.sum(lq1_ref[...] * lk1_ref[...], axis=-1, keepdims=True))
           - jnp.exp(jnp.sum(lq2_ref[...] * lk2_ref[...], axis=-1, keepdims=True))
           + lam_init)

    first_feat = lax.broadcasted_iota(jnp.int32, (LANES, 1), 0) < DIFF_HEAD_DIM
    key_c = lax.broadcasted_iota(jnp.int32, (tk, 2 * tq), 0) // CHUNK
    qry_c = lax.broadcasted_iota(jnp.int32, (tk, 2 * tq), 1) % tq // CHUNK
    diag_allowed = key_c <= qry_c
    ones_rows = jnp.ones((ONES_ROWS, tk), BF16)

    def scores(job):
        g, j, _ = job
        kt = k_ref[0, 0, pl.ds(pl.multiple_of(j * tk, tk), tk), :]
        return jnp.dot(kt, q2_ref[g], preferred_element_type=F32)

    def softmax_update(job, s):
        g, _, masked = job
        if masked:
            s = jnp.where(diag_allowed, s, -jnp.inf)
        m_prev = m_ref[g]
        m_new = jnp.maximum(m_prev, jnp.max(s, axis=0, keepdims=True))
        m_ref[g] = m_new
        return jnp.exp2(m_prev - m_new), jnp.exp2(s - m_new).astype(BF16)

    def accumulate(job, alpha, p):
        g, j, _ = job
        vte = jnp.concatenate([vt_ref[0, 0, j], ones_rows], axis=0)
        acc_ref[g] = alpha * acc_ref[g] + jnp.dot(vte, p, preferred_element_type=F32)

    def run_stream(jobs, lookahead, pending_job, state):
        s_cur, alpha, p = state
        pend = (pending_job, alpha, p)
        for i, job in enumerate(jobs):
            nxt = jobs[i + 1] if i + 1 < len(jobs) else lookahead
            s_nxt = scores(nxt) if nxt is not None else None
            alpha, p = softmax_update(job, s_cur)
            accumulate(*pend)
            pend = (job, alpha, p)
            s_cur = s_nxt
        if lookahead is None:
            accumulate(*pend)
            return None
        return s_cur, pend[1], pend[2]

    def start_super_block(si):
        base = pl.multiple_of(si * SUPER_ATT, SUPER_ATT)
        for g in range(ng):
            qb = q_ref[0, 0, pl.ds(base + g * tq, tq), :]
            qt = qb.astype(F32).T
            zero = jnp.zeros_like(qt)
            q2_ref[g, :, :tq] = jnp.where(first_feat, qt, zero).astype(BF16)
            q2_ref[g, :, tq:] = jnp.where(first_feat, zero, qt).astype(BF16)
        m_ref[...] = jnp.full(m_ref.shape, -jnp.inf, F32)
        acc_ref[...] = jnp.zeros(acc_ref.shape, F32)
        return scores((0, 0, False))

    def super_block(si, s_first):
        base = pl.multiple_of(si * SUPER_ATT, SUPER_ATT)

        def full_tiles(jj, state):
            j0 = jj * FULL_TILES_PER_ITER
            jobs = [(g, j0 + u, False) for u in range(FULL_TILES_PER_ITER) for g in range(ng)]
            return run_stream(jobs, (0, j0 + FULL_TILES_PER_ITER, False),
                              (ng - 1, jnp.maximum(j0 - 1, 0), False), state)

        state = (s_first, jnp.ones((1, 2 * tq), F32), jnp.zeros((tk, 2 * tq), BF16))
        n_full = si * (tiles_per_super // FULL_TILES_PER_ITER)
        state = lax.fori_loop(0, n_full, full_tiles, state)

        j0 = si * tiles_per_super
        jobs = [(g, j0 + bt, g == bt) for bt in range(tiles_per_super) for g in range(bt, ng)]
        run_stream(jobs, None, (ng - 1, jnp.maximum(j0 - 1, 0), False), state)

        outs = []
        for g in range(ng):
            acc = acc_ref[g]
            o_all = acc[:DIFF_V_DIM] * (1.0 / acc[DIFF_V_DIM:DIFF_V_DIM + 1])
            o = o_all[:, :tq] - lam * o_all[:, tq:]
            ms = jnp.mean(o * o, axis=0, keepdims=True)
            on = (o * lax.rsqrt(ms + EPS)) * g_ref[...]
            outs.append((on * (1.0 - lam_init)).T)
        s_next = start_super_block(jnp.minimum(si + 1, n_super - 1))
        for g in range(ng):
            rows = pl.ds(base + g * tq, tq)
            o_ref[0, rows, :] = (outs[g] * zb_ref[0, rows, :].astype(F32)).astype(BF16)
        return s_next

    lax.fori_loop(0, n_super, super_block, start_super_block(0))


def _attention(q, k, vt, zb, subln_g_col, lq1, lk1, lq2, lk2, lam_init):
    b, h, s, _ = q.shape
    head_spec = pl.BlockSpec((1, 1, s, LANES), lambda bi, hi: (bi, hi, 0, 0))
    vt_spec = pl.BlockSpec((1, 1, s // TK_ATT, DIFF_V_DIM, TK_ATT),
                           lambda bi, hi: (bi, hi, 0, 0, 0))
    col_spec = pl.BlockSpec((1, s, LANES), lambda bi, hi: (bi, 0, hi))
    vec64 = pl.BlockSpec((1, DIFF_HEAD_DIM), lambda bi, hi: (0, 0))
    return pl.pallas_call(
        functools.partial(_attn_kernel, lam_init),
        grid=(b, h),
        in_specs=[head_spec, head_spec, vt_spec, col_spec,
                  pl.BlockSpec((DIFF_V_DIM, 1), lambda bi, hi: (0, 0)),
                  vec64, vec64, vec64, vec64],
        out_specs=col_spec,
        out_shape=jax.ShapeDtypeStruct((b, s, h * LANES), BF16),
        scratch_shapes=[pltpu.VMEM((G_ATT, LANES, 2 * TQ_ATT), BF16),
                        pltpu.VMEM((G_ATT, DIFF_V_DIM + ONES_ROWS, 2 * TQ_ATT), F32),
                        pltpu.VMEM((G_ATT, 1, 2 * TQ_ATT), F32)],
        compiler_params=pltpu.CompilerParams(
            dimension_semantics=("arbitrary", "arbitrary"), vmem_limit_bytes=VMEM_LIMIT),
        name="diff_attention",
    )(q, k, vt, zb, subln_g_col, lq1, lk1, lq2, lk2)


def _out_kernel(ya_ref, yb_ref, x_ref, gate_ref, w_ref, fg_ref, o_ref, wb_ref):
    _cast_weight_once(w_ref, wb_ref)
    y = jnp.dot(ya_ref[0], wb_ref[:SEC, :], preferred_element_type=F32)
    y = y + jnp.dot(yb_ref[0], wb_ref[SEC:, :], preferred_element_type=F32)
    xo = x_ref[0] + gate_ref[0] * y
    ms = jnp.mean(xo * xo, axis=-1, keepdims=True)
    o_ref[0] = (xo * lax.rsqrt(ms + EPS)) * fg_ref[...]


def _out_projection(ya, yb, x, gate, w_out, final_g):
    b, s, d = x.shape
    tm = TM_PROJ
    return pl.pallas_call(
        _out_kernel,
        grid=(b, s // tm),
        in_specs=[
            pl.BlockSpec((1, tm, SEC), lambda bi, i: (bi, i, 0)),
            pl.BlockSpec((1, tm, SEC), lambda bi, i: (bi, i, 0)),
            pl.BlockSpec((1, tm, d), lambda bi, i: (bi, i, 0)),
            pl.BlockSpec((1, 1, d), lambda bi, i: (bi, 0, 0)),
            pl.BlockSpec((2 * SEC, d), lambda bi, i: (0, 0), pipeline_mode=pl.Buffered(1)),
            pl.BlockSpec((1, d), lambda bi, i: (0, 0)),
        ],
        out_specs=pl.BlockSpec((1, tm, d), lambda bi, i: (bi, i, 0)),
        out_shape=jax.ShapeDtypeStruct((b, s, d), F32),
        scratch_shapes=[pltpu.VMEM((2 * SEC, d), BF16)],
        compiler_params=pltpu.CompilerParams(
            dimension_semantics=("arbitrary", "arbitrary"), vmem_limit_bytes=VMEM_LIMIT),
        name="out_proj_residual_norm",
    )(ya, yb, x, gate, w_out, final_g)


def kernel(x, c, positions, w_ada, b_ada, norm_g, w_in, gm_ln_g, gm_ln_b, gm_ws, gm_bs,
           lam_q1, lam_k1, lam_q2, lam_k2, diff_subln_g, w_out, final_g):
    b, s, d = x.shape
    depth = w_ada.shape[0]
    half = DIFF_HEAD_DIM // 2
    inv = ROPE_THETA ** (-jnp.arange(half, dtype=F32) * 2.0 / DIFF_HEAD_DIM)
    inv128 = jnp.tile(inv, LANES // half)[None, :]
    pos4 = positions.reshape(b, s // TM_PROJ, TM_PROJ // LANES, LANES)
    c8 = jnp.zeros((8, d), F32).at[:b].set(c)
    assert depth == 1
    l = 0
    mod = _modulation(c8, w_ada[l], b_ada[l][None, :])[:b]
    shift = mod[:, None, :d]
    scale = mod[:, None, d:2 * d]
    gate = mod[:, None, 2 * d:]
    ya, q, k, vt, zb = _projection(
        x, pos4, scale, shift, norm_g[l][None, :], w_in[l], inv128,
        gm_ln_g[l][None, :], gm_ln_b[l][None, :], gm_ws[l], gm_bs[l][:, :, None])
    lam_init = 0.8 - 0.6 * math.exp(-0.3 * l)
    yb = _attention(q, k, vt, zb, diff_subln_g[l][:, None], lam_q1[l][None, :],
                    lam_k1[l][None, :], lam_q2[l][None, :], lam_k2[l][None, :], lam_init)
    return _out_projection(ya, yb, x, gate, w_out[l], final_g[None, :])
```

```python
import functools
import math

import jax
import jax.numpy as jnp
from jax import lax
from jax.experimental import pallas as pl
from jax.experimental.pallas import tpu as pltpu

F32 = jnp.float32
BF16 = jnp.bfloat16

EPS = 1e-6
CHUNK = 64
GM_GROUPS = 4
GM_GROUP_DIM = 128
GM_WINDOW = 128
DIFF_HEADS = 4
DIFF_HEAD_DIM = 64
DIFF_V_DIM = 128
ROPE_THETA = 10000.0
SEC = 512
N_SEC = 7
LOG2E = 1.4426950408889634

LANES = 128
VMEM_LIMIT = 48 * 1024 * 1024

TM_PROJ = 512
WEIGHT_CAST_ROWS = 128
TQ_ATT = 256
TK_ATT = 256
G_ATT = 8
SUPER_ATT = TQ_ATT * G_ATT
FULL_TILES_PER_ITER = 4
ONES_ROWS = 16
assert SUPER_ATT % TK_ATT == 0 and TQ_ATT % CHUNK == 0 and TK_ATT % CHUNK == 0
assert (SUPER_ATT // TK_ATT) % FULL_TILES_PER_ITER == 0
assert TM_PROJ % TK_ATT == 0 and TM_PROJ % TQ_ATT == 0


def _silu(x):
    return x * (1.0 / (1.0 + jnp.exp(-x)))


def _gelu(x):
    return 0.5 * x * (1.0 + lax.erf(x * (1.0 / math.sqrt(2.0))))


def _mod_kernel(c_ref, w_ref, b_ref, o_ref):
    cs = _silu(c_ref[...]).astype(BF16)
    w = w_ref[...].astype(BF16)
    o_ref[...] = jnp.dot(cs, w, preferred_element_type=F32) + b_ref[...]


def _modulation(c8, w_ada, b_ada):
    d, n = w_ada.shape
    tn = 1024
    return pl.pallas_call(
        _mod_kernel,
        grid=(n // tn,),
        in_specs=[
            pl.BlockSpec((8, d), lambda j: (0, 0)),
            pl.BlockSpec((d, tn), lambda j: (0, j)),
            pl.BlockSpec((1, tn), lambda j: (0, j)),
        ],
        out_specs=pl.BlockSpec((8, tn), lambda j: (0, j)),
        out_shape=jax.ShapeDtypeStruct((8, n), F32),
        compiler_params=pltpu.CompilerParams(
            dimension_semantics=("arbitrary",), vmem_limit_bytes=VMEM_LIMIT),
        name="adaln_mod",
    )(c8, w_ada, b_ada)


def _cast_weight_once(w_ref, wb_ref):
    first = (pl.program_id(0) == 0) & (pl.program_id(1) == 0)

    @pl.when(first)
    def _():
        def body(i, c):
            rows = pl.ds(pl.multiple_of(i * WEIGHT_CAST_ROWS, WEIGHT_CAST_ROWS), WEIGHT_CAST_ROWS)
            wb_ref[rows, :] = w_ref[rows, :].astype(BF16)
            return c
        lax.fori_loop(0, w_ref.shape[0] // WEIGHT_CAST_ROWS, body, 0)


def _rope_heads(t, cos, sin_signed, lane_lo):
    outs = []
    for hh in range(SEC // LANES):
        xh = t[:, hh * LANES:(hh + 1) * LANES]
        partner = jnp.where(lane_lo, pltpu.roll(xh, LANES - 32, 1), pltpu.roll(xh, 32, 1))
        outs.append(xh * cos + partner * sin_signed)
    return outs


def _proj_kernel(x_ref, pos_ref, scale_ref, shift_ref, ng_ref, w_ref, inv_ref,
                 lng_ref, lnb_ref, ws_ref, bs_ref,
                 ya_ref, qt_ref, k_ref, vt_ref, zb_ref, vln_ref, sv_ref, wb_ref):
    tm = x_ref.shape[1]
    _cast_weight_once(w_ref, wb_ref)

    pos = pos_ref[0, 0].astype(F32)
    angs = []
    for a in range(tm // LANES):
        col = jnp.broadcast_to(pos[a:a + 1, :], (LANES, LANES)).T
        angs.append(col * inv_ref[...])
    ang = jnp.concatenate(angs, axis=0)
    cos = jnp.cos(ang)
    sin = jnp.sin(ang)
    lane = lax.broadcasted_iota(jnp.int32, (1, LANES), 1)
    lane_lo = (lane % DIFF_HEAD_DIM) < (DIFF_HEAD_DIM // 2)
    sin_signed = jnp.where(lane_lo, -sin, sin)

    x = x_ref[0]
    ms = jnp.mean(x * x, axis=-1, keepdims=True)
    xn = (x * lax.rsqrt(ms + EPS)) * ng_ref[...]
    h = xn * (1.0 + scale_ref[0]) + shift_ref[0]
    hb = h.astype(BF16)

    def proj(sec):
        return jnp.dot(hb, wb_ref[:, sec * SEC:(sec + 1) * SEC], preferred_element_type=F32)

    pv = proj(1)
    pu = proj(0)
    v = _gelu(pv)
    for g in range(GM_GROUPS):
        vg = v[:, g * LANES:(g + 1) * LANES]
        mu = jnp.mean(vg, axis=-1, keepdims=True)
        vc = vg - mu
        var = jnp.mean(vc * vc, axis=-1, keepdims=True)
        vn = vc * lax.rsqrt(var + EPS)
        vn = vn * lng_ref[:, g * LANES:(g + 1) * LANES] + lnb_ref[:, g * LANES:(g + 1) * LANES]
        vln_ref[:, g * LANES:(g + 1) * LANES] = vn.astype(BF16)
    pz = proj(2)
    gu = _gelu(pu)

    row_c = lax.broadcasted_iota(jnp.int32, (GM_WINDOW, GM_WINDOW), 0) // CHUNK
    col_c = lax.broadcasted_iota(jnp.int32, (GM_WINDOW, GM_WINDOW), 1) // CHUNK
    tri = row_c >= col_c
    n_win = tm // GM_WINDOW
    for g in range(GM_GROUPS):
        wsm = jnp.where(tri, ws_ref[g], 0.0).astype(BF16)
        cols = slice(g * LANES, (g + 1) * LANES)
        vwin = jnp.concatenate(
            [vln_ref[w * GM_WINDOW:(w + 1) * GM_WINDOW, cols] for w in range(n_win)], axis=1)
        svg = jnp.dot(wsm, vwin, preferred_element_type=F32) + bs_ref[g]
        for w in range(n_win):
            sv_ref[w * GM_WINDOW:(w + 1) * GM_WINDOW, cols] = svg[:, w * LANES:(w + 1) * LANES]

    pq = proj(3)
    ya_ref[0] = (gu * sv_ref[...] * _silu(pz)).astype(BF16)
    pk = proj(4)
    qscale = (DIFF_HEAD_DIM ** -0.5) * LOG2E
    for hh, qh in enumerate(_rope_heads(pq, cos, sin_signed, lane_lo)):
        for sub in range(tm // TQ_ATT):
            blk = qh[sub * TQ_ATT:(sub + 1) * TQ_ATT, :] * qscale
            qt_ref[0, hh, sub] = blk.T.astype(BF16)
    pvd = proj(5)
    for hh, kh in enumerate(_rope_heads(pk, cos, sin_signed, lane_lo)):
        k_ref[0, hh] = kh.astype(BF16)
    pzb = proj(6)
    for hh in range(DIFF_HEADS):
        for sub in range(tm // TK_ATT):
            blk = pvd[sub * TK_ATT:(sub + 1) * TK_ATT, hh * LANES:(hh + 1) * LANES]
            vt_ref[0, hh, sub] = blk.T.astype(BF16)
    zb_ref[0] = _silu(pzb).astype(BF16)


def _projection(x, pos4, scale, shift, norm_g, w_in, inv128, ln_g, ln_b, ws, bs3):
    b, s, d = x.shape
    tm = TM_PROJ
    head_shape = jax.ShapeDtypeStruct((b, DIFF_HEADS, s, LANES), BF16)
    row_shape = jax.ShapeDtypeStruct((b, s, SEC), BF16)
    head_spec = pl.BlockSpec((1, DIFF_HEADS, tm, LANES), lambda bi, i: (bi, 0, i, 0))
    row_spec = pl.BlockSpec((1, tm, SEC), lambda bi, i: (bi, i, 0))
    vt_shape = jax.ShapeDtypeStruct((b, DIFF_HEADS, s // TK_ATT, DIFF_V_DIM, TK_ATT), BF16)
    vt_spec = pl.BlockSpec((1, DIFF_HEADS, tm // TK_ATT, DIFF_V_DIM, TK_ATT),
                           lambda bi, i: (bi, 0, i, 0, 0))
    qt_shape = jax.ShapeDtypeStruct((b, DIFF_HEADS, s // TQ_ATT, LANES, TQ_ATT), BF16)
    qt_spec = pl.BlockSpec((1, DIFF_HEADS, tm // TQ_ATT, LANES, TQ_ATT),
                           lambda bi, i: (bi, 0, i, 0, 0))
    const2 = lambda bi, i: (0, 0)
    const3 = lambda bi, i: (0, 0, 0)
    return pl.pallas_call(
        _proj_kernel,
        grid=(b, s // tm),
        in_specs=[
            pl.BlockSpec((1, tm, d), lambda bi, i: (bi, i, 0)),
            pl.BlockSpec((1, 1, tm // LANES, LANES), lambda bi, i: (bi, i, 0, 0)),
            pl.BlockSpec((1, 1, d), lambda bi, i: (bi, 0, 0)),
            pl.BlockSpec((1, 1, d), lambda bi, i: (bi, 0, 0)),
            pl.BlockSpec((1, d), const2),
            pl.BlockSpec((d, N_SEC * SEC), const2,
                         pipeline_mode=pl.Buffered(1)),
            pl.BlockSpec((1, LANES), const2),
            pl.BlockSpec((1, SEC), const2),
            pl.BlockSpec((1, SEC), const2),
            pl.BlockSpec((GM_GROUPS, GM_WINDOW, GM_WINDOW), const3),
            pl.BlockSpec((GM_GROUPS, GM_WINDOW, 1), const3),
        ],
        out_specs=[row_spec, qt_spec, head_spec, vt_spec, row_spec],
        out_shape=[row_shape, qt_shape, head_shape, vt_shape, row_shape],
        scratch_shapes=[pltpu.VMEM((tm, SEC), BF16), pltpu.VMEM((tm, SEC), F32),
                        pltpu.VMEM((d, N_SEC * SEC), BF16)],
        compiler_params=pltpu.CompilerParams(
            dimension_semantics=("arbitrary", "arbitrary"), vmem_limit_bytes=VMEM_LIMIT),
        name="in_proj_gmlp_rope",
    )(x, pos4, scale, shift, norm_g, w_in, inv128, ln_g, ln_b, ws, bs3)


def _attn_kernel(lam_init, qt_ref, k_ref, vt_ref, zb_ref, g_ref, lq1_ref, lk1_ref, lq2_ref,
                 lk2_ref, o_ref, q2_ref, acc_ref, m_ref):
    s_len = k_ref.shape[2]
    tq, tk, ng = TQ_ATT, TK_ATT, G_ATT
    n_super = s_len // SUPER_ATT
    tiles_per_super = SUPER_ATT // tk

    lam = (jnp.exp(jnp.sum(lq1_ref[...] * lk1_ref[...], axis=-1, keepdims=True))
           - jnp.exp(jnp.sum(lq2_ref[...] * lk2_ref[...], axis=-1, keepdims=True))
           + lam_init)

    first_feat = lax.broadcasted_iota(jnp.int32, (LANES, 1), 0) < DIFF_HEAD_DIM
    key_pos = lax.broadcasted_iota(jnp.int32, (tk, 2 * tq), 0)
    qry_pos = lax.broadcasted_iota(jnp.int32, (tk, 2 * tq), 1) % tq
    ones_rows = jnp.ones((ONES_ROWS, tk), BF16)

    def allowed(q_off):
        return key_pos // CHUNK <= (qry_pos + q_off) // CHUNK

    def scores(job):
        g, j, _ = job
        kt = k_ref[0, 0, pl.ds(pl.multiple_of(j * tk, tk), tk), :]
        return jnp.dot(kt, q2_ref[g], preferred_element_type=F32)

    def softmax_update(job, s):
        g, _, q_off = job
        if q_off is not None:
            s = jnp.where(allowed(q_off), s, -jnp.inf)
        m_prev = m_ref[g]
        m_new = jnp.maximum(m_prev, jnp.max(s, axis=0, keepdims=True))
        m_ref[g] = m_new
        return jnp.exp2(m_prev - m_new), jnp.exp2(s - m_new).astype(BF16)

    def accumulate(job, alpha, p):
        g, j, _ = job
        vte = jnp.concatenate([vt_ref[0, 0, j], ones_rows], axis=0)
        acc_ref[g] = alpha * acc_ref[g] + jnp.dot(vte, p, preferred_element_type=F32)

    def run_stream(jobs, lookahead, pending_job, state):
        s_cur, alpha, p = state
        pend = (pending_job, alpha, p)
        for i, job in enumerate(jobs):
            nxt = jobs[i + 1] if i + 1 < len(jobs) else lookahead
            s_nxt = scores(nxt) if nxt is not None else None
            alpha, p = softmax_update(job, s_cur)
            accumulate(*pend)
            pend = (job, alpha, p)
            s_cur = s_nxt
        if lookahead is None:
            accumulate(*pend)
            return None
        return s_cur, pend[1], pend[2]

    def load_queries(g, si):
        qt = qt_ref[0, 0, si * ng + g]
        zero = jnp.zeros_like(qt)
        q2_ref[g, :, :tq] = jnp.where(first_feat, qt, zero)
        q2_ref[g, :, tq:] = jnp.where(first_feat, zero, qt)

    def reset_chain(g):
        m_ref[g] = jnp.full(m_ref.shape[1:], -jnp.inf, F32)
        acc_ref[g] = jnp.zeros(acc_ref.shape[1:], F32)

    def finish_chain(g, si):
        acc = acc_ref[g]
        o_all = acc[:DIFF_V_DIM] * (1.0 / acc[DIFF_V_DIM:DIFF_V_DIM + 1])
        o = o_all[:, :tq] - lam * o_all[:, tq:]
        ms = jnp.mean(o * o, axis=0, keepdims=True)
        on = (o * lax.rsqrt(ms + EPS)) * g_ref[...]
        on = (on * (1.0 - lam_init)).T
        rows = pl.ds(pl.multiple_of(si * SUPER_ATT, SUPER_ATT) + g * tq, tq)
        o_ref[0, rows, :] = (on * zb_ref[0, rows, :].astype(F32)).astype(BF16)
        reset_chain(g)

    def full_tiles(jj, state):
        j0 = jj * FULL_TILES_PER_ITER
        jobs = [(g, j0 + u, None) for u in range(FULL_TILES_PER_ITER) for g in range(ng)]
        return run_stream(jobs, (0, j0 + FULL_TILES_PER_ITER, None),
                          (ng - 1, jnp.maximum(j0 - 1, 0), None), state)

    def super_block(si, s_first):
        state = (s_first, jnp.ones((1, 2 * tq), F32), jnp.zeros((tk, 2 * tq), BF16))
        n_full = si * (tiles_per_super // FULL_TILES_PER_ITER)
        state = lax.fori_loop(0, n_full, full_tiles, state)

        j0 = si * tiles_per_super
        jobs = []
        for bt in range(tiles_per_super):
            for g in range(ng):
                q_off = g * tq - bt * tk
                if q_off + tq <= 0:
                    continue
                needs_mask = q_off // CHUNK < (tk - 1) // CHUNK
                jobs.append((g, j0 + bt, q_off if needs_mask else None))
        assert jobs[0][0] == 0 and jobs[-1][0] == ng - 1
        run_stream(jobs, None, (ng - 1, jnp.maximum(j0 - 1, 0), None), state)
        for g in range(ng):
            finish_chain(g, si)
        si_next = jnp.minimum(si + 1, n_super - 1)
        for g in range(ng):
            load_queries(g, si_next)
        return scores((0, 0, None))

    for g0 in range(ng):
        load_queries(g0, 0)
        reset_chain(g0)

    lax.fori_loop(0, n_super, super_block, scores((0, 0, None)))


def _attention(qt, k, vt, zb, subln_g_col, lq1, lk1, lq2, lk2, lam_init):
    b, h, s, _ = k.shape
    head_spec = pl.BlockSpec((1, 1, s, LANES), lambda bi, hi: (bi, hi, 0, 0))
    qt_spec = pl.BlockSpec((1, 1, s // TQ_ATT, LANES, TQ_ATT), lambda bi, hi: (bi, hi, 0, 0, 0))
    vt_spec = pl.BlockSpec((1, 1, s // TK_ATT, DIFF_V_DIM, TK_ATT),
                           lambda bi, hi: (bi, hi, 0, 0, 0))
    col_spec = pl.BlockSpec((1, s, LANES), lambda bi, hi: (bi, 0, hi))
    vec64 = pl.BlockSpec((1, DIFF_HEAD_DIM), lambda bi, hi: (0, 0))
    return pl.pallas_call(
        functools.partial(_attn_kernel, lam_init),
        grid=(b, h),
        in_specs=[qt_spec, head_spec, vt_spec, col_spec,
                  pl.BlockSpec((DIFF_V_DIM, 1), lambda bi, hi: (0, 0)),
                  vec64, vec64, vec64, vec64],
        out_specs=col_spec,
        out_shape=jax.ShapeDtypeStruct((b, s, h * LANES), BF16),
        scratch_shapes=[pltpu.VMEM((G_ATT, LANES, 2 * TQ_ATT), BF16),
                        pltpu.VMEM((G_ATT, DIFF_V_DIM + ONES_ROWS, 2 * TQ_ATT), F32),
                        pltpu.VMEM((G_ATT, 1, 2 * TQ_ATT), F32)],
        compiler_params=pltpu.CompilerParams(
            dimension_semantics=("arbitrary", "arbitrary"), vmem_limit_bytes=VMEM_LIMIT),
        name="diff_attention",
    )(qt, k, vt, zb, subln_g_col, lq1, lk1, lq2, lk2)


def _out_kernel(ya_ref, yb_ref, x_ref, gate_ref, w_ref, fg_ref, o_ref, wb_ref):
    _cast_weight_once(w_ref, wb_ref)
    y = jnp.dot(ya_ref[0], wb_ref[:SEC, :], preferred_element_type=F32)
    y = y + jnp.dot(yb_ref[0], wb_ref[SEC:, :], preferred_element_type=F32)
    xo = x_ref[0] + gate_ref[0] * y
    ms = jnp.mean(xo * xo, axis=-1, keepdims=True)
    o_ref[0] = (xo * lax.rsqrt(ms + EPS)) * fg_ref[...]


def _out_projection(ya, yb, x, gate, w_out, final_g):
    b, s, d = x.shape
    tm = TM_PROJ
    return pl.pallas_call(
        _out_kernel,
        grid=(b, s // tm),
        in_specs=[
            pl.BlockSpec((1, tm, SEC), lambda bi, i: (bi, i, 0)),
            pl.BlockSpec((1, tm, SEC), lambda bi, i: (bi, i, 0)),
            pl.BlockSpec((1, tm, d), lambda bi, i: (bi, i, 0)),
            pl.BlockSpec((1, 1, d), lambda bi, i: (bi, 0, 0)),
            pl.BlockSpec((2 * SEC, d), lambda bi, i: (0, 0), pipeline_mode=pl.Buffered(1)),
            pl.BlockSpec((1, d), lambda bi, i: (0, 0)),
        ],
        out_specs=pl.BlockSpec((1, tm, d), lambda bi, i: (bi, i, 0)),
        out_shape=jax.ShapeDtypeStruct((b, s, d), F32),
        scratch_shapes=[pltpu.VMEM((2 * SEC, d), BF16)],
        compiler_params=pltpu.CompilerParams(
            dimension_semantics=("arbitrary", "arbitrary"), vmem_limit_bytes=VMEM_LIMIT),
        name="out_proj_residual_norm",
    )(ya, yb, x, gate, w_out, final_g)


def kernel(x, c, positions, w_ada, b_ada, norm_g, w_in, gm_ln_g, gm_ln_b, gm_ws, gm_bs,
           lam_q1, lam_k1, lam_q2, lam_k2, diff_subln_g, w_out, final_g):
    b, s, d = x.shape
    depth = w_ada.shape[0]
    half = DIFF_HEAD_DIM // 2
    inv = ROPE_THETA ** (-jnp.arange(half, dtype=F32) * 2.0 / DIFF_HEAD_DIM)
    inv128 = jnp.tile(inv, LANES // half)[None, :]
    pos4 = positions.reshape(b, s // TM_PROJ, TM_PROJ // LANES, LANES)
    c8 = jnp.zeros((8, d), F32).at[:b].set(c)
    assert depth == 1
    l = 0
    mod = _modulation(c8, w_ada[l], b_ada[l][None, :])[:b]
    shift = mod[:, None, :d]
    scale = mod[:, None, d:2 * d]
    gate = mod[:, None, 2 * d:]
    ya, q, k, vt, zb = _projection(
        x, pos4, scale, shift, norm_g[l][None, :], w_in[l], inv128,
        gm_ln_g[l][None, :], gm_ln_b[l][None, :], gm_ws[l], gm_bs[l][:, :, None])
    lam_init = 0.8 - 0.6 * math.exp(-0.3 * l)
    yb = _attention(q, k, vt, zb, diff_subln_g[l][:, None], lam_q1[l][None, :],
                    lam_k1[l][None, :], lam_q2[l][None, :], lam_k2[l][None, :], lam_init)
    return _out_projection(ya, yb, x, gate, w_out[l], final_g[None, :])
```

```python
import functools
import math

import jax
import jax.numpy as jnp
from jax import lax
from jax.experimental import pallas as pl
from jax.experimental.pallas import tpu as pltpu

F32 = jnp.float32
BF16 = jnp.bfloat16

EPS = 1e-6
CHUNK = 64
GM_GROUPS = 4
GM_GROUP_DIM = 128
GM_WINDOW = 128
DIFF_HEADS = 4
DIFF_HEAD_DIM = 64
DIFF_V_DIM = 128
ROPE_THETA = 10000.0
SEC = 512
N_SEC = 7
LOG2E = 1.4426950408889634

LANES = 128
VMEM_LIMIT = 48 * 1024 * 1024

TM_PROJ = 512
TM_OUT = 1024
WEIGHT_CAST_ROWS = 128
TQ_ATT = 256
TK_ATT = 256
G_ATT = 8
SUPER_ATT = TQ_ATT * G_ATT
FULL_TILES_PER_ITER = 4
ONES_ROWS = 16
assert SUPER_ATT % TK_ATT == 0 and TQ_ATT % CHUNK == 0 and TK_ATT % CHUNK == 0
assert (SUPER_ATT // TK_ATT) % FULL_TILES_PER_ITER == 0
assert TM_PROJ % TK_ATT == 0 and TM_PROJ % TQ_ATT == 0


def _silu(x):
    h = 0.5 * x
    return h + h * jnp.tanh(h)


def _gelu(x):
    return 0.5 * x * (1.0 + lax.erf(x * (1.0 / math.sqrt(2.0))))


def _mod_kernel(c_ref, w_ref, b_ref, o_ref):
    cs = _silu(c_ref[...]).astype(BF16)
    w = w_ref[...].astype(BF16)
    o_ref[...] = jnp.dot(cs, w, preferred_element_type=F32) + b_ref[...]


def _modulation(c8, w_ada, b_ada):
    d, n = w_ada.shape
    tn = 1024
    return pl.pallas_call(
        _mod_kernel,
        grid=(n // tn,),
        in_specs=[
            pl.BlockSpec((8, d), lambda j: (0, 0)),
            pl.BlockSpec((d, tn), lambda j: (0, j)),
            pl.BlockSpec((1, tn), lambda j: (0, j)),
        ],
        out_specs=pl.BlockSpec((8, tn), lambda j: (0, j)),
        out_shape=jax.ShapeDtypeStruct((8, n), F32),
        compiler_params=pltpu.CompilerParams(
            dimension_semantics=("arbitrary",), vmem_limit_bytes=VMEM_LIMIT),
        name="adaln_mod",
    )(c8, w_ada, b_ada)


def _cast_weight_once(w_ref, wb_ref):
    first = (pl.program_id(0) == 0) & (pl.program_id(1) == 0)

    @pl.when(first)
    def _():
        def body(i, c):
            rows = pl.ds(pl.multiple_of(i * WEIGHT_CAST_ROWS, WEIGHT_CAST_ROWS), WEIGHT_CAST_ROWS)
            wb_ref[rows, :] = w_ref[rows, :].astype(BF16)
            return c
        lax.fori_loop(0, w_ref.shape[0] // WEIGHT_CAST_ROWS, body, 0)


def _rope_heads(t, cos, sin_signed, lane_lo):
    outs = []
    for hh in range(SEC // LANES):
        xh = t[:, hh * LANES:(hh + 1) * LANES]
        partner = jnp.where(lane_lo, pltpu.roll(xh, LANES - 32, 1), pltpu.roll(xh, 32, 1))
        outs.append(xh * cos + partner * sin_signed)
    return outs


def _proj_kernel(x_ref, pos_ref, scale_ref, shift_ref, ng_ref, w_ref, inv_ref,
                 lng_ref, lnb_ref, ws_ref, bs_ref,
                 ya_ref, qt_ref, k_ref, vt_ref, zb_ref, vln_ref, sv_ref, wb_ref):
    tm = x_ref.shape[1]
    _cast_weight_once(w_ref, wb_ref)

    pos = pos_ref[0, 0].astype(F32)
    angs = []
    for a in range(tm // LANES):
        col = jnp.broadcast_to(pos[a:a + 1, :], (LANES, LANES)).T
        angs.append(col * inv_ref[...])
    ang = jnp.concatenate(angs, axis=0)
    cos = jnp.cos(ang)
    sin = jnp.sin(ang)
    lane = lax.broadcasted_iota(jnp.int32, (1, LANES), 1)
    lane_lo = (lane % DIFF_HEAD_DIM) < (DIFF_HEAD_DIM // 2)
    sin_signed = jnp.where(lane_lo, -sin, sin)

    x = x_ref[0]
    ms = jnp.mean(x * x, axis=-1, keepdims=True)
    xn = (x * lax.rsqrt(ms + EPS)) * ng_ref[...]
    h = xn * (1.0 + scale_ref[0]) + shift_ref[0]
    hb = h.astype(BF16)

    def proj(sec):
        return jnp.dot(hb, wb_ref[:, sec * SEC:(sec + 1) * SEC], preferred_element_type=F32)

    pv = proj(1)
    pu = proj(0)
    v = _gelu(pv)
    for g in range(GM_GROUPS):
        vg = v[:, g * LANES:(g + 1) * LANES]
        mu = jnp.mean(vg, axis=-1, keepdims=True)
        vc = vg - mu
        var = jnp.mean(vc * vc, axis=-1, keepdims=True)
        vn = vc * lax.rsqrt(var + EPS)
        vn = vn * lng_ref[:, g * LANES:(g + 1) * LANES] + lnb_ref[:, g * LANES:(g + 1) * LANES]
        vln_ref[:, g * LANES:(g + 1) * LANES] = vn.astype(BF16)
    pz = proj(2)
    gu = _gelu(pu)

    row_c = lax.broadcasted_iota(jnp.int32, (GM_WINDOW, GM_WINDOW), 0) // CHUNK
    col_c = lax.broadcasted_iota(jnp.int32, (GM_WINDOW, GM_WINDOW), 1) // CHUNK
    tri = row_c >= col_c
    n_win = tm // GM_WINDOW
    for g in range(GM_GROUPS):
        wsm = jnp.where(tri, ws_ref[g], 0.0).astype(BF16)
        cols = slice(g * LANES, (g + 1) * LANES)
        vwin = jnp.concatenate(
            [vln_ref[w * GM_WINDOW:(w + 1) * GM_WINDOW, cols] for w in range(n_win)], axis=1)
        svg = jnp.dot(wsm, vwin, preferred_element_type=F32) + bs_ref[g]
        for w in range(n_win):
            sv_ref[w * GM_WINDOW:(w + 1) * GM_WINDOW, cols] = svg[:, w * LANES:(w + 1) * LANES]

    pq = proj(3)
    ya_ref[0] = (gu * sv_ref[...] * _silu(pz)).astype(BF16)
    pk = proj(4)
    qscale = (DIFF_HEAD_DIM ** -0.5) * LOG2E
    for hh, qh in enumerate(_rope_heads(pq, cos, sin_signed, lane_lo)):
        for sub in range(tm // TQ_ATT):
            blk = qh[sub * TQ_ATT:(sub + 1) * TQ_ATT, :] * qscale
            qt_ref[0, hh, sub] = blk.T.astype(BF16)
    pvd = proj(5)
    for hh, kh in enumerate(_rope_heads(pk, cos, sin_signed, lane_lo)):
        k_ref[0, hh] = kh.astype(BF16)
    pzb = proj(6)
    for hh in range(DIFF_HEADS):
        for sub in range(tm // TK_ATT):
            blk = pvd[sub * TK_ATT:(sub + 1) * TK_ATT, hh * LANES:(hh + 1) * LANES]
            vt_ref[0, hh, sub] = blk.T.astype(BF16)
    zb_ref[0] = _silu(pzb).astype(BF16)


def _projection(x, pos4, scale, shift, norm_g, w_in, inv128, ln_g, ln_b, ws, bs3):
    b, s, d = x.shape
    tm = TM_PROJ
    head_shape = jax.ShapeDtypeStruct((b, DIFF_HEADS, s, LANES), BF16)
    row_shape = jax.ShapeDtypeStruct((b, s, SEC), BF16)
    head_spec = pl.BlockSpec((1, DIFF_HEADS, tm, LANES), lambda bi, i: (bi, 0, i, 0))
    row_spec = pl.BlockSpec((1, tm, SEC), lambda bi, i: (bi, i, 0))
    vt_shape = jax.ShapeDtypeStruct((b, DIFF_HEADS, s // TK_ATT, DIFF_V_DIM, TK_ATT), BF16)
    vt_spec = pl.BlockSpec((1, DIFF_HEADS, tm // TK_ATT, DIFF_V_DIM, TK_ATT),
                           lambda bi, i: (bi, 0, i, 0, 0))
    qt_shape = jax.ShapeDtypeStruct((b, DIFF_HEADS, s // TQ_ATT, LANES, TQ_ATT), BF16)
    qt_spec = pl.BlockSpec((1, DIFF_HEADS, tm // TQ_ATT, LANES, TQ_ATT),
                           lambda bi, i: (bi, 0, i, 0, 0))
    const2 = lambda bi, i: (0, 0)
    const3 = lambda bi, i: (0, 0, 0)
    return pl.pallas_call(
        _proj_kernel,
        grid=(b, s // tm),
        in_specs=[
            pl.BlockSpec((1, tm, d), lambda bi, i: (bi, i, 0)),
            pl.BlockSpec((1, 1, tm // LANES, LANES), lambda bi, i: (bi, i, 0, 0)),
            pl.BlockSpec((1, 1, d), lambda bi, i: (bi, 0, 0)),
            pl.BlockSpec((1, 1, d), lambda bi, i: (bi, 0, 0)),
            pl.BlockSpec((1, d), const2),
            pl.BlockSpec((d, N_SEC * SEC), const2,
                         pipeline_mode=pl.Buffered(1)),
            pl.BlockSpec((1, LANES), const2),
            pl.BlockSpec((1, SEC), const2),
            pl.BlockSpec((1, SEC), const2),
            pl.BlockSpec((GM_GROUPS, GM_WINDOW, GM_WINDOW), const3),
            pl.BlockSpec((GM_GROUPS, GM_WINDOW, 1), const3),
        ],
        out_specs=[row_spec, qt_spec, head_spec, vt_spec, row_spec],
        out_shape=[row_shape, qt_shape, head_shape, vt_shape, row_shape],
        scratch_shapes=[pltpu.VMEM((tm, SEC), BF16), pltpu.VMEM((tm, SEC), F32),
                        pltpu.VMEM((d, N_SEC * SEC), BF16)],
        compiler_params=pltpu.CompilerParams(
            dimension_semantics=("arbitrary", "arbitrary"), vmem_limit_bytes=VMEM_LIMIT),
        name="in_proj_gmlp_rope",
    )(x, pos4, scale, shift, norm_g, w_in, inv128, ln_g, ln_b, ws, bs3)


def _attn_kernel(lam_init, qt_ref, k_ref, vt_ref, zb_ref, g_ref, lq1_ref, lk1_ref, lq2_ref,
                 lk2_ref, o_ref, q2_ref, acc_ref, m_ref):
    s_len = k_ref.shape[2]
    tq, tk, ng = TQ_ATT, TK_ATT, G_ATT
    n_super = s_len // SUPER_ATT
    tiles_per_super = SUPER_ATT // tk

    lam = (jnp.exp(jnp.sum(lq1_ref[...] * lk1_ref[...], axis=-1, keepdims=True))
           - jnp.exp(jnp.sum(lq2_ref[...] * lk2_ref[...], axis=-1, keepdims=True))
           + lam_init)

    first_feat = lax.broadcasted_iota(jnp.int32, (LANES, 1), 0) < DIFF_HEAD_DIM
    key_pos = lax.broadcasted_iota(jnp.int32, (tk, 2 * tq), 0)
    qry_pos = lax.broadcasted_iota(jnp.int32, (tk, 2 * tq), 1) % tq
    ones_rows = jnp.ones((ONES_ROWS, tk), BF16)

    def allowed(q_off):
        return key_pos // CHUNK <= (qry_pos + q_off) // CHUNK

    def scores(job):
        g, j, _ = job
        kt = k_ref[0, 0, pl.ds(pl.multiple_of(j * tk, tk), tk), :]
        return jnp.dot(kt, q2_ref[g], preferred_element_type=F32)

    def softmax_update(job, s):
        g, _, q_off = job
        if q_off is not None:
            s = jnp.where(allowed(q_off), s, -jnp.inf)
        m_prev = m_ref[g]
        m_new = jnp.maximum(m_prev, jnp.max(s, axis=0, keepdims=True))
        m_ref[g] = m_new
        return jnp.exp2(m_prev - m_new), jnp.exp2(s - m_new).astype(BF16)

    def accumulate(job, alpha, p):
        g, j, _ = job
        vte = jnp.concatenate([vt_ref[0, 0, j], ones_rows], axis=0)
        acc_ref[g] = alpha * acc_ref[g] + jnp.dot(vte, p, preferred_element_type=F32)

    def run_stream(jobs, lookahead, pending_job, state):
        s_cur, alpha, p = state
        pend = (pending_job, alpha, p)
        for i, job in enumerate(jobs):
            nxt = jobs[i + 1] if i + 1 < len(jobs) else lookahead
            s_nxt = scores(nxt) if nxt is not None else None
            alpha, p = softmax_update(job, s_cur)
            accumulate(*pend)
            pend = (job, alpha, p)
            s_cur = s_nxt
        if lookahead is None:
            accumulate(*pend)
            return None
        return s_cur, pend[1], pend[2]

    def load_queries(g, si):
        qt = qt_ref[0, 0, si * ng + g]
        zero = jnp.zeros_like(qt)
        q2_ref[g, :, :tq] = jnp.where(first_feat, qt, zero)
        q2_ref[g, :, tq:] = jnp.where(first_feat, zero, qt)

    def reset_chain(g):
        m_ref[g] = jnp.full(m_ref.shape[1:], -jnp.inf, F32)
        acc_ref[g] = jnp.zeros(acc_ref.shape[1:], F32)

    def finish_chain(g, si):
        acc = acc_ref[g]
        o_all = acc[:DIFF_V_DIM] * (1.0 / acc[DIFF_V_DIM:DIFF_V_DIM + 1])
        o = o_all[:, :tq] - lam * o_all[:, tq:]
        ms = jnp.mean(o * o, axis=0, keepdims=True)
        on = (o * lax.rsqrt(ms + EPS)) * g_ref[...]
        on = (on * (1.0 - lam_init)).T
        rows = pl.ds(pl.multiple_of(si * SUPER_ATT, SUPER_ATT) + g * tq, tq)
        o_ref[0, rows, :] = (on * zb_ref[0, rows, :].astype(F32)).astype(BF16)
        reset_chain(g)

    def full_tiles(jj, state):
        j0 = jj * FULL_TILES_PER_ITER
        jobs = [(g, j0 + u, None) for u in range(FULL_TILES_PER_ITER) for g in range(ng)]
        return run_stream(jobs, (0, j0 + FULL_TILES_PER_ITER, None),
                          (ng - 1, jnp.maximum(j0 - 1, 0), None), state)

    def super_block(si, s_first):
        state = (s_first, jnp.ones((1, 2 * tq), F32), jnp.zeros((tk, 2 * tq), BF16))
        n_full = si * (tiles_per_super // FULL_TILES_PER_ITER)
        state = lax.fori_loop(0, n_full, full_tiles, state)

        j0 = si * tiles_per_super
        jobs = []
        for bt in range(tiles_per_super):
            for g in range(ng):
                q_off = g * tq - bt * tk
                if q_off + tq <= 0:
                    continue
                needs_mask = q_off // CHUNK < (tk - 1) // CHUNK
                jobs.append((g, j0 + bt, q_off if needs_mask else None))
        assert jobs[0][0] == 0 and jobs[-1][0] == ng - 1
        run_stream(jobs, None, (ng - 1, jnp.maximum(j0 - 1, 0), None), state)
        for g in range(ng):
            finish_chain(g, si)
        si_next = jnp.minimum(si + 1, n_super - 1)
        for g in range(ng):
            load_queries(g, si_next)
        return scores((0, 0, None))

    for g0 in range(ng):
        load_queries(g0, 0)
        reset_chain(g0)

    lax.fori_loop(0, n_super, super_block, scores((0, 0, None)))


def _attention(qt, k, vt, zb, subln_g_col, lq1, lk1, lq2, lk2, lam_init):
    b, h, s, _ = k.shape
    head_spec = pl.BlockSpec((1, 1, s, LANES), lambda bi, hi: (bi, hi, 0, 0))
    qt_spec = pl.BlockSpec((1, 1, s // TQ_ATT, LANES, TQ_ATT), lambda bi, hi: (bi, hi, 0, 0, 0))
    vt_spec = pl.BlockSpec((1, 1, s // TK_ATT, DIFF_V_DIM, TK_ATT),
                           lambda bi, hi: (bi, hi, 0, 0, 0))
    col_spec = pl.BlockSpec((1, s, LANES), lambda bi, hi: (bi, 0, hi))
    vec64 = pl.BlockSpec((1, DIFF_HEAD_DIM), lambda bi, hi: (0, 0))
    return pl.pallas_call(
        functools.partial(_attn_kernel, lam_init),
        grid=(b, h),
        in_specs=[qt_spec, head_spec, vt_spec, col_spec,
                  pl.BlockSpec((DIFF_V_DIM, 1), lambda bi, hi: (0, 0)),
                  vec64, vec64, vec64, vec64],
        out_specs=col_spec,
        out_shape=jax.ShapeDtypeStruct((b, s, h * LANES), BF16),
        scratch_shapes=[pltpu.VMEM((G_ATT, LANES, 2 * TQ_ATT), BF16),
                        pltpu.VMEM((G_ATT, DIFF_V_DIM + ONES_ROWS, 2 * TQ_ATT), F32),
                        pltpu.VMEM((G_ATT, 1, 2 * TQ_ATT), F32)],
        compiler_params=pltpu.CompilerParams(
            dimension_semantics=("arbitrary", "arbitrary"), vmem_limit_bytes=VMEM_LIMIT),
        name="diff_attention",
    )(qt, k, vt, zb, subln_g_col, lq1, lk1, lq2, lk2)


def _out_kernel(ya_ref, yb_ref, x_ref, gate_ref, w_ref, fg_ref, o_ref, wb_ref):
    _cast_weight_once(w_ref, wb_ref)
    y = jnp.dot(ya_ref[0], wb_ref[:SEC, :], preferred_element_type=F32)
    y = y + jnp.dot(yb_ref[0], wb_ref[SEC:, :], preferred_element_type=F32)
    xo = x_ref[0] + gate_ref[0] * y
    ms = jnp.mean(xo * xo, axis=-1, keepdims=True)
    o_ref[0] = (xo * lax.rsqrt(ms + EPS)) * fg_ref[...]


def _out_projection(ya, yb, x, gate, w_out, final_g):
    b, s, d = x.shape
    tm = TM_OUT
    return pl.pallas_call(
        _out_kernel,
        grid=(b, s // tm),
        in_specs=[
            pl.BlockSpec((1, tm, SEC), lambda bi, i: (bi, i, 0)),
            pl.BlockSpec((1, tm, SEC), lambda bi, i: (bi, i, 0)),
            pl.BlockSpec((1, tm, d), lambda bi, i: (bi, i, 0)),
            pl.BlockSpec((1, 1, d), lambda bi, i: (bi, 0, 0)),
            pl.BlockSpec((2 * SEC, d), lambda bi, i: (0, 0), pipeline_mode=pl.Buffered(1)),
            pl.BlockSpec((1, d), lambda bi, i: (0, 0)),
        ],
        out_specs=pl.BlockSpec((1, tm, d), lambda bi, i: (bi, i, 0)),
        out_shape=jax.ShapeDtypeStruct((b, s, d), F32),
        scratch_shapes=[pltpu.VMEM((2 * SEC, d), BF16)],
        compiler_params=pltpu.CompilerParams(
            dimension_semantics=("arbitrary", "arbitrary"), vmem_limit_bytes=VMEM_LIMIT),
        name="out_proj_residual_norm",
    )(ya, yb, x, gate, w_out, final_g)


def kernel(x, c, positions, w_ada, b_ada, norm_g, w_in, gm_ln_g, gm_ln_b, gm_ws, gm_bs,
           lam_q1, lam_k1, lam_q2, lam_k2, diff_subln_g, w_out, final_g):
    b, s, d = x.shape
    depth = w_ada.shape[0]
    half = DIFF_HEAD_DIM // 2
    inv = ROPE_THETA ** (-jnp.arange(half, dtype=F32) * 2.0 / DIFF_HEAD_DIM)
    inv128 = jnp.tile(inv, LANES // half)[None, :]
    pos4 = positions.reshape(b, s // TM_PROJ, TM_PROJ // LANES, LANES)
    c8 = jnp.zeros((8, d), F32).at[:b].set(c)
    assert depth == 1
    l = 0
    mod = _modulation(c8, w_ada[l], b_ada[l][None, :])[:b]
    shift = mod[:, None, :d]
    scale = mod[:, None, d:2 * d]
    gate = mod[:, None, 2 * d:]
    ya, q, k, vt, zb = _projection(
        x, pos4, scale, shift, norm_g[l][None, :], w_in[l], inv128,
        gm_ln_g[l][None, :], gm_ln_b[l][None, :], gm_ws[l], gm_bs[l][:, :, None])
    lam_init = 0.8 - 0.6 * math.exp(-0.3 * l)
    yb = _attention(q, k, vt, zb, diff_subln_g[l][:, None], lam_q1[l][None, :],
                    lam_k1[l][None, :], lam_q2[l][None, :], lam_k2[l][None, :], lam_init)
    return _out_projection(ya, yb, x, gate, w_out[l], final_g[None, :])
```

```python
import functools
import math

import jax
import jax.numpy as jnp
from jax import lax
from jax.experimental import pallas as pl
from jax.experimental.pallas import tpu as pltpu

F32 = jnp.float32
BF16 = jnp.bfloat16

EPS = 1e-6
CHUNK = 64
GM_GROUPS = 4
GM_GROUP_DIM = 128
GM_WINDOW = 128
DIFF_HEADS = 4
DIFF_HEAD_DIM = 64
DIFF_V_DIM = 128
ROPE_THETA = 10000.0
SEC = 512
N_SEC = 7
LOG2E = 1.4426950408889634

LANES = 128
VMEM_LIMIT = 48 * 1024 * 1024

TM_PROJ = 512
TM_OUT = 1024
WEIGHT_CAST_ROWS = 128
TQ_ATT = 256
TK_ATT = 256
G_ATT = 8
SUPER_ATT = TQ_ATT * G_ATT
FULL_TILES_PER_ITER = 8
ONES_ROWS = 16
assert SUPER_ATT % TK_ATT == 0 and TQ_ATT % CHUNK == 0 and TK_ATT % CHUNK == 0
assert (SUPER_ATT // TK_ATT) % FULL_TILES_PER_ITER == 0
assert TM_PROJ % TK_ATT == 0 and TM_PROJ % TQ_ATT == 0


def _silu(x):
    h = 0.5 * x
    return h + h * jnp.tanh(h)


def _gelu(x):
    return 0.5 * x * (1.0 + lax.erf(x * (1.0 / math.sqrt(2.0))))


def _mod_kernel(c_ref, w_ref, b_ref, o_ref):
    cs = _silu(c_ref[...]).astype(BF16)
    w = w_ref[...].astype(BF16)
    o_ref[...] = jnp.dot(cs, w, preferred_element_type=F32) + b_ref[...]


def _modulation(c8, w_ada, b_ada):
    d, n = w_ada.shape
    tn = 1024
    return pl.pallas_call(
        _mod_kernel,
        grid=(n // tn,),
        in_specs=[
            pl.BlockSpec((8, d), lambda j: (0, 0)),
            pl.BlockSpec((d, tn), lambda j: (0, j)),
            pl.BlockSpec((1, tn), lambda j: (0, j)),
        ],
        out_specs=pl.BlockSpec((8, tn), lambda j: (0, j)),
        out_shape=jax.ShapeDtypeStruct((8, n), F32),
        compiler_params=pltpu.CompilerParams(
            dimension_semantics=("arbitrary",), vmem_limit_bytes=VMEM_LIMIT),
        name="adaln_mod",
    )(c8, w_ada, b_ada)


def _cast_weight_once(w_ref, wb_ref):
    first = (pl.program_id(0) == 0) & (pl.program_id(1) == 0)

    @pl.when(first)
    def _():
        def body(i, c):
            rows = pl.ds(pl.multiple_of(i * WEIGHT_CAST_ROWS, WEIGHT_CAST_ROWS), WEIGHT_CAST_ROWS)
            wb_ref[rows, :] = w_ref[rows, :].astype(BF16)
            return c
        lax.fori_loop(0, w_ref.shape[0] // WEIGHT_CAST_ROWS, body, 0)


def _rope_heads(t, cos, sin_signed, lane_lo):
    outs = []
    for hh in range(SEC // LANES):
        xh = t[:, hh * LANES:(hh + 1) * LANES]
        partner = jnp.where(lane_lo, pltpu.roll(xh, LANES - 32, 1), pltpu.roll(xh, 32, 1))
        outs.append(xh * cos + partner * sin_signed)
    return outs


def _proj_kernel(x_ref, pos_ref, scale_ref, shift_ref, ng_ref, w_ref, inv_ref,
                 lng_ref, lnb_ref, ws_ref, bs_ref,
                 ya_ref, qt_ref, k_ref, vt_ref, zb_ref, vln_ref, sv_ref, wb_ref, wsm_ref):
    tm = x_ref.shape[1]
    _cast_weight_once(w_ref, wb_ref)

    @pl.when((pl.program_id(0) == 0) & (pl.program_id(1) == 0))
    def _():
        row_c = lax.broadcasted_iota(jnp.int32, (GM_WINDOW, GM_WINDOW), 0) // CHUNK
        col_c = lax.broadcasted_iota(jnp.int32, (GM_WINDOW, GM_WINDOW), 1) // CHUNK
        for g in range(GM_GROUPS):
            wsm_ref[g] = jnp.where(row_c >= col_c, ws_ref[g], 0.0).astype(BF16)

    pos = pos_ref[0, 0].astype(F32)
    angs = []
    for a in range(tm // LANES):
        col = jnp.broadcast_to(pos[a:a + 1, :], (LANES, LANES)).T
        angs.append(col * inv_ref[...])
    ang = jnp.concatenate(angs, axis=0)
    cos = jnp.cos(ang)
    sin = jnp.sin(ang)
    lane = lax.broadcasted_iota(jnp.int32, (1, LANES), 1)
    lane_lo = (lane % DIFF_HEAD_DIM) < (DIFF_HEAD_DIM // 2)
    sin_signed = jnp.where(lane_lo, -sin, sin)

    x = x_ref[0]
    ms = jnp.mean(x * x, axis=-1, keepdims=True)
    xn = (x * lax.rsqrt(ms + EPS)) * ng_ref[...]
    h = xn * (1.0 + scale_ref[0]) + shift_ref[0]
    hb = h.astype(BF16)

    def proj(sec):
        return jnp.dot(hb, wb_ref[:, sec * SEC:(sec + 1) * SEC], preferred_element_type=F32)

    pv = proj(1)
    pu = proj(0)
    v = _gelu(pv)
    n_win = tm // GM_WINDOW
    for g in range(GM_GROUPS):
        cols = slice(g * LANES, (g + 1) * LANES)
        vg = v[:, cols]
        mu = jnp.mean(vg, axis=-1, keepdims=True)
        vc = vg - mu
        var = jnp.mean(vc * vc, axis=-1, keepdims=True)
        vn = vc * lax.rsqrt(var + EPS)
        vn = vn * lng_ref[:, cols] + lnb_ref[:, cols]
        vln_ref[:, cols] = vn.astype(BF16)
        vwin = jnp.concatenate(
            [vln_ref[w * GM_WINDOW:(w + 1) * GM_WINDOW, cols] for w in range(n_win)], axis=1)
        svg = jnp.dot(wsm_ref[g], vwin, preferred_element_type=F32) + bs_ref[g]
        for w in range(n_win):
            sv_ref[w * GM_WINDOW:(w + 1) * GM_WINDOW, cols] = svg[:, w * LANES:(w + 1) * LANES]
    pz = proj(2)
    gu = _gelu(pu)

    pq = proj(3)
    ya_ref[0] = (gu * sv_ref[...] * _silu(pz)).astype(BF16)
    pk = proj(4)
    qscale = (DIFF_HEAD_DIM ** -0.5) * LOG2E
    for hh, qh in enumerate(_rope_heads(pq, cos, sin_signed, lane_lo)):
        for sub in range(tm // TQ_ATT):
            blk = qh[sub * TQ_ATT:(sub + 1) * TQ_ATT, :] * qscale
            qt_ref[0, hh, sub] = blk.T.astype(BF16)
    pvd = proj(5)
    for hh, kh in enumerate(_rope_heads(pk, cos, sin_signed, lane_lo)):
        k_ref[0, hh] = kh.astype(BF16)
    pzb = proj(6)
    for hh in range(DIFF_HEADS):
        for sub in range(tm // TK_ATT):
            blk = pvd[sub * TK_ATT:(sub + 1) * TK_ATT, hh * LANES:(hh + 1) * LANES]
            vt_ref[0, hh, sub] = blk.T.astype(BF16)
    zb_ref[0] = _silu(pzb).astype(BF16)


def _projection(x, pos4, scale, shift, norm_g, w_in, inv128, ln_g, ln_b, ws, bs3):
    b, s, d = x.shape
    tm = TM_PROJ
    head_shape = jax.ShapeDtypeStruct((b, DIFF_HEADS, s, LANES), BF16)
    row_shape = jax.ShapeDtypeStruct((b, s, SEC), BF16)
    head_spec = pl.BlockSpec((1, DIFF_HEADS, tm, LANES), lambda bi, i: (bi, 0, i, 0))
    row_spec = pl.BlockSpec((1, tm, SEC), lambda bi, i: (bi, i, 0))
    vt_shape = jax.ShapeDtypeStruct((b, DIFF_HEADS, s // TK_ATT, DIFF_V_DIM, TK_ATT), BF16)
    vt_spec = pl.BlockSpec((1, DIFF_HEADS, tm // TK_ATT, DIFF_V_DIM, TK_ATT),
                           lambda bi, i: (bi, 0, i, 0, 0))
    qt_shape = jax.ShapeDtypeStruct((b, DIFF_HEADS, s // TQ_ATT, LANES, TQ_ATT), BF16)
    qt_spec = pl.BlockSpec((1, DIFF_HEADS, tm // TQ_ATT, LANES, TQ_ATT),
                           lambda bi, i: (bi, 0, i, 0, 0))
    const2 = lambda bi, i: (0, 0)
    const3 = lambda bi, i: (0, 0, 0)
    return pl.pallas_call(
        _proj_kernel,
        grid=(b, s // tm),
        in_specs=[
            pl.BlockSpec((1, tm, d), lambda bi, i: (bi, i, 0)),
            pl.BlockSpec((1, 1, tm // LANES, LANES), lambda bi, i: (bi, i, 0, 0)),
            pl.BlockSpec((1, 1, d), lambda bi, i: (bi, 0, 0)),
            pl.BlockSpec((1, 1, d), lambda bi, i: (bi, 0, 0)),
            pl.BlockSpec((1, d), const2),
            pl.BlockSpec((d, N_SEC * SEC), const2,
                         pipeline_mode=pl.Buffered(1)),
            pl.BlockSpec((1, LANES), const2),
            pl.BlockSpec((1, SEC), const2),
            pl.BlockSpec((1, SEC), const2),
            pl.BlockSpec((GM_GROUPS, GM_WINDOW, GM_WINDOW), const3),
            pl.BlockSpec((GM_GROUPS, GM_WINDOW, 1), const3),
        ],
        out_specs=[row_spec, qt_spec, head_spec, vt_spec, row_spec],
        out_shape=[row_shape, qt_shape, head_shape, vt_shape, row_shape],
        scratch_shapes=[pltpu.VMEM((tm, SEC), BF16), pltpu.VMEM((tm, SEC), F32),
                        pltpu.VMEM((d, N_SEC * SEC), BF16),
                        pltpu.VMEM((GM_GROUPS, GM_WINDOW, GM_WINDOW), BF16)],
        compiler_params=pltpu.CompilerParams(
            dimension_semantics=("arbitrary", "arbitrary"), vmem_limit_bytes=VMEM_LIMIT),
        name="in_proj_gmlp_rope",
    )(x, pos4, scale, shift, norm_g, w_in, inv128, ln_g, ln_b, ws, bs3)


def _attn_kernel(lam_init, qt_ref, k_ref, vt_ref, zb_ref, g_ref, lq1_ref, lk1_ref, lq2_ref,
                 lk2_ref, o_ref, q2_ref, acc_ref, m_ref):
    s_len = k_ref.shape[2]
    tq, tk, ng = TQ_ATT, TK_ATT, G_ATT
    n_super = s_len // SUPER_ATT
    tiles_per_super = SUPER_ATT // tk

    lam = (jnp.exp(jnp.sum(lq1_ref[...] * lk1_ref[...], axis=-1, keepdims=True))
           - jnp.exp(jnp.sum(lq2_ref[...] * lk2_ref[...], axis=-1, keepdims=True))
           + lam_init)

    first_feat = lax.broadcasted_iota(jnp.int32, (LANES, 1), 0) < DIFF_HEAD_DIM
    key_pos = lax.broadcasted_iota(jnp.int32, (tk, 2 * tq), 0)
    qry_pos = lax.broadcasted_iota(jnp.int32, (tk, 2 * tq), 1) % tq
    ones_rows = jnp.ones((ONES_ROWS, tk), BF16)

    def allowed(q_off):
        return key_pos // CHUNK <= (qry_pos + q_off) // CHUNK

    def scores(job):
        g, j, _ = job
        kt = k_ref[0, 0, pl.ds(pl.multiple_of(j * tk, tk), tk), :]
        return jnp.dot(kt, q2_ref[g], preferred_element_type=F32)

    def softmax_update(job, s):
        g, _, q_off = job
        if q_off is not None:
            s = jnp.where(allowed(q_off), s, -jnp.inf)
        m_prev = m_ref[g]
        m_new = jnp.maximum(m_prev, jnp.max(s, axis=0, keepdims=True))
        m_ref[g] = m_new
        return jnp.exp2(m_prev - m_new), jnp.exp2(s - m_new).astype(BF16)

    def accumulate(job, alpha, p):
        g, j, _ = job
        vte = jnp.concatenate([vt_ref[0, 0, j], ones_rows], axis=0)
        acc_ref[g] = alpha * acc_ref[g] + jnp.dot(vte, p, preferred_element_type=F32)

    def run_stream(jobs, lookahead, pending_job, state):
        s_cur, alpha, p = state
        pend = (pending_job, alpha, p)
        for i, job in enumerate(jobs):
            nxt = jobs[i + 1] if i + 1 < len(jobs) else lookahead
            s_nxt = scores(nxt) if nxt is not None else None
            alpha, p = softmax_update(job, s_cur)
            accumulate(*pend)
            pend = (job, alpha, p)
            s_cur = s_nxt
        if lookahead is None:
            accumulate(*pend)
            return None
        return s_cur, pend[1], pend[2]

    def load_queries(g, si):
        qt = qt_ref[0, 0, si * ng + g]
        zero = jnp.zeros_like(qt)
        q2_ref[g, :, :tq] = jnp.where(first_feat, qt, zero)
        q2_ref[g, :, tq:] = jnp.where(first_feat, zero, qt)

    def reset_chain(g):
        m_ref[g] = jnp.full(m_ref.shape[1:], -jnp.inf, F32)
        acc_ref[g] = jnp.zeros(acc_ref.shape[1:], F32)

    def finish_chain(g, si):
        acc = acc_ref[g]
        o_all = acc[:DIFF_V_DIM] * (1.0 / acc[DIFF_V_DIM:DIFF_V_DIM + 1])
        o = o_all[:, :tq] - lam * o_all[:, tq:]
        ms = jnp.mean(o * o, axis=0, keepdims=True)
        on = (o * lax.rsqrt(ms + EPS)) * g_ref[...]
        on = (on * (1.0 - lam_init)).T
        rows = pl.ds(pl.multiple_of(si * SUPER_ATT, SUPER_ATT) + g * tq, tq)
        o_ref[0, rows, :] = (on * zb_ref[0, rows, :].astype(F32)).astype(BF16)
        reset_chain(g)

    def full_tiles(jj, state):
        j0 = jj * FULL_TILES_PER_ITER
        jobs = [(g, j0 + u, None) for u in range(FULL_TILES_PER_ITER) for g in range(ng)]
        return run_stream(jobs, (0, j0 + FULL_TILES_PER_ITER, None),
                          (ng - 1, jnp.maximum(j0 - 1, 0), None), state)

    def super_block(si, s_first):
        state = (s_first, jnp.ones((1, 2 * tq), F32), jnp.zeros((tk, 2 * tq), BF16))
        n_full = si * (tiles_per_super // FULL_TILES_PER_ITER)
        state = lax.fori_loop(0, n_full, full_tiles, state)

        j0 = si * tiles_per_super
        jobs = []
        for bt in range(tiles_per_super):
            for g in range(ng):
                q_off = g * tq - bt * tk
                if q_off + tq <= 0:
                    continue
                needs_mask = q_off // CHUNK < (tk - 1) // CHUNK
                jobs.append((g, j0 + bt, q_off if needs_mask else None))
        assert jobs[0][0] == 0 and jobs[-1][0] == ng - 1
        run_stream(jobs, None, (ng - 1, jnp.maximum(j0 - 1, 0), None), state)
        for g in range(ng):
            finish_chain(g, si)
        si_next = jnp.minimum(si + 1, n_super - 1)
        for g in range(ng):
            load_queries(g, si_next)
        return scores((0, 0, None))

    for g0 in range(ng):
        load_queries(g0, 0)
        reset_chain(g0)

    lax.fori_loop(0, n_super, super_block, scores((0, 0, None)))


def _attention(qt, k, vt, zb, subln_g_col, lq1, lk1, lq2, lk2, lam_init):
    b, h, s, _ = k.shape
    head_spec = pl.BlockSpec((1, 1, s, LANES), lambda bi, hi: (bi, hi, 0, 0))
    qt_spec = pl.BlockSpec((1, 1, s // TQ_ATT, LANES, TQ_ATT), lambda bi, hi: (bi, hi, 0, 0, 0))
    vt_spec = pl.BlockSpec((1, 1, s // TK_ATT, DIFF_V_DIM, TK_ATT),
                           lambda bi, hi: (bi, hi, 0, 0, 0))
    col_spec = pl.BlockSpec((1, s, LANES), lambda bi, hi: (bi, 0, hi))
    vec64 = pl.BlockSpec((1, DIFF_HEAD_DIM), lambda bi, hi: (0, 0))
    return pl.pallas_call(
        functools.partial(_attn_kernel, lam_init),
        grid=(b, h),
        in_specs=[qt_spec, head_spec, vt_spec, col_spec,
                  pl.BlockSpec((DIFF_V_DIM, 1), lambda bi, hi: (0, 0)),
                  vec64, vec64, vec64, vec64],
        out_specs=col_spec,
        out_shape=jax.ShapeDtypeStruct((b, s, h * LANES), BF16),
        scratch_shapes=[pltpu.VMEM((G_ATT, LANES, 2 * TQ_ATT), BF16),
                        pltpu.VMEM((G_ATT, DIFF_V_DIM + ONES_ROWS, 2 * TQ_ATT), F32),
                        pltpu.VMEM((G_ATT, 1, 2 * TQ_ATT), F32)],
        compiler_params=pltpu.CompilerParams(
            dimension_semantics=("arbitrary", "arbitrary"), vmem_limit_bytes=VMEM_LIMIT),
        name="diff_attention",
    )(qt, k, vt, zb, subln_g_col, lq1, lk1, lq2, lk2)


def _out_kernel(ya_ref, yb_ref, x_ref, gate_ref, w_ref, fg_ref, o_ref, wb_ref):
    _cast_weight_once(w_ref, wb_ref)
    y = jnp.dot(ya_ref[0], wb_ref[:SEC, :], preferred_element_type=F32)
    y = y + jnp.dot(yb_ref[0], wb_ref[SEC:, :], preferred_element_type=F32)
    xo = x_ref[0] + gate_ref[0] * y
    ms = jnp.mean(xo * xo, axis=-1, keepdims=True)
    o_ref[0] = (xo * lax.rsqrt(ms + EPS)) * fg_ref[...]


def _out_projection(ya, yb, x, gate, w_out, final_g):
    b, s, d = x.shape
    tm = TM_OUT
    return pl.pallas_call(
        _out_kernel,
        grid=(b, s // tm),
        in_specs=[
            pl.BlockSpec((1, tm, SEC), lambda bi, i: (bi, i, 0)),
            pl.BlockSpec((1, tm, SEC), lambda bi, i: (bi, i, 0)),
            pl.BlockSpec((1, tm, d), lambda bi, i: (bi, i, 0)),
            pl.BlockSpec((1, 1, d), lambda bi, i: (bi, 0, 0)),
            pl.BlockSpec((2 * SEC, d), lambda bi, i: (0, 0), pipeline_mode=pl.Buffered(1)),
            pl.BlockSpec((1, d), lambda bi, i: (0, 0)),
        ],
        out_specs=pl.BlockSpec((1, tm, d), lambda bi, i: (bi, i, 0)),
        out_shape=jax.ShapeDtypeStruct((b, s, d), F32),
        scratch_shapes=[pltpu.VMEM((2 * SEC, d), BF16)],
        compiler_params=pltpu.CompilerParams(
            dimension_semantics=("arbitrary", "arbitrary"), vmem_limit_bytes=VMEM_LIMIT),
        name="out_proj_residual_norm",
    )(ya, yb, x, gate, w_out, final_g)


def kernel(x, c, positions, w_ada, b_ada, norm_g, w_in, gm_ln_g, gm_ln_b, gm_ws, gm_bs,
           lam_q1, lam_k1, lam_q2, lam_k2, diff_subln_g, w_out, final_g):
    b, s, d = x.shape
    depth = w_ada.shape[0]
    half = DIFF_HEAD_DIM // 2
    inv = ROPE_THETA ** (-jnp.arange(half, dtype=F32) * 2.0 / DIFF_HEAD_DIM)
    inv128 = jnp.tile(inv, LANES // half)[None, :]
    pos4 = positions.reshape(b, s // TM_PROJ, TM_PROJ // LANES, LANES)
    c8 = jnp.zeros((8, d), F32).at[:b].set(c)
    assert depth == 1
    l = 0
    mod = _modulation(c8, w_ada[l], b_ada[l][None, :])[:b]
    shift = mod[:, None, :d]
    scale = mod[:, None, d:2 * d]
    gate = mod[:, None, 2 * d:]
    ya, q, k, vt, zb = _projection(
        x, pos4, scale, shift, norm_g[l][None, :], w_in[l], inv128,
        gm_ln_g[l][None, :], gm_ln_b[l][None, :], gm_ws[l], gm_bs[l][:, :, None])
    lam_init = 0.8 - 0.6 * math.exp(-0.3 * l)
    yb = _attention(q, k, vt, zb, diff_subln_g[l][:, None], lam_q1[l][None, :],
                    lam_k1[l][None, :], lam_q2[l][None, :], lam_k2[l][None, :], lam_init)
    return _out_projection(ya, yb, x, gate, w_out[l], final_g[None, :])
```

```python
import functools
import math

import jax
import jax.numpy as jnp
from jax import lax
from jax.experimental import pallas as pl
from jax.experimental.pallas import tpu as pltpu

F32 = jnp.float32
BF16 = jnp.bfloat16

EPS = 1e-6
CHUNK = 64
GM_GROUPS = 4
GM_GROUP_DIM = 128
GM_WINDOW = 128
DIFF_HEADS = 4
DIFF_HEAD_DIM = 64
DIFF_V_DIM = 128
ROPE_THETA = 10000.0
SEC = 512
N_SEC = 7
LOG2E = 1.4426950408889634

LANES = 128
VMEM_LIMIT = 48 * 1024 * 1024

TM_PROJ = 512
TM_OUT = 1024
OUT_ROW_SPLIT = 256
WEIGHT_CAST_ROWS = 128
TQ_ATT = 256
TK_ATT = 256
G_ATT = 8
SUPER_ATT = TQ_ATT * G_ATT
FULL_TILES_PER_ITER = 8
ONES_ROWS = 16
assert SUPER_ATT % TK_ATT == 0 and TQ_ATT % CHUNK == 0 and TK_ATT % CHUNK == 0
assert (SUPER_ATT // TK_ATT) % FULL_TILES_PER_ITER == 0
assert TM_PROJ % TK_ATT == 0 and TM_PROJ % TQ_ATT == 0


def _silu(x):
    h = 0.5 * x
    return h + h * jnp.tanh(h)


def _gelu(x):
    return 0.5 * x * (1.0 + lax.erf(x * (1.0 / math.sqrt(2.0))))


def _mod_kernel(c_ref, w_ref, b_ref, o_ref):
    cs = _silu(c_ref[...]).astype(BF16)
    w = w_ref[...].astype(BF16)
    o_ref[...] = jnp.dot(cs, w, preferred_element_type=F32) + b_ref[...]


def _modulation(c8, w_ada, b_ada):
    d, n = w_ada.shape
    tn = 1024
    return pl.pallas_call(
        _mod_kernel,
        grid=(n // tn,),
        in_specs=[
            pl.BlockSpec((8, d), lambda j: (0, 0)),
            pl.BlockSpec((d, tn), lambda j: (0, j)),
            pl.BlockSpec((1, tn), lambda j: (0, j)),
        ],
        out_specs=pl.BlockSpec((8, tn), lambda j: (0, j)),
        out_shape=jax.ShapeDtypeStruct((8, n), F32),
        compiler_params=pltpu.CompilerParams(
            dimension_semantics=("arbitrary",), vmem_limit_bytes=VMEM_LIMIT),
        name="adaln_mod",
    )(c8, w_ada, b_ada)


def _cast_weight_once(w_ref, wb_ref):
    first = (pl.program_id(0) == 0) & (pl.program_id(1) == 0)

    @pl.when(first)
    def _():
        def body(i, c):
            rows = pl.ds(pl.multiple_of(i * WEIGHT_CAST_ROWS, WEIGHT_CAST_ROWS), WEIGHT_CAST_ROWS)
            wb_ref[rows, :] = w_ref[rows, :].astype(BF16)
            return c
        lax.fori_loop(0, w_ref.shape[0] // WEIGHT_CAST_ROWS, body, 0)


def _rope_heads(t, cos, sin_signed, lane_lo):
    outs = []
    for hh in range(SEC // LANES):
        xh = t[:, hh * LANES:(hh + 1) * LANES]
        partner = jnp.where(lane_lo, pltpu.roll(xh, LANES - 32, 1), pltpu.roll(xh, 32, 1))
        outs.append(xh * cos + partner * sin_signed)
    return outs


def _proj_kernel(x_ref, pos_ref, scale_ref, shift_ref, ng_ref, w_ref, inv_ref,
                 lng_ref, lnb_ref, ws_ref, bs_ref,
                 ya_ref, qt_ref, k_ref, vt_ref, zb_ref, vln_ref, sv_ref, wb_ref, wsm_ref):
    tm = x_ref.shape[1]
    _cast_weight_once(w_ref, wb_ref)

    @pl.when((pl.program_id(0) == 0) & (pl.program_id(1) == 0))
    def _():
        row_c = lax.broadcasted_iota(jnp.int32, (GM_WINDOW, GM_WINDOW), 0) // CHUNK
        col_c = lax.broadcasted_iota(jnp.int32, (GM_WINDOW, GM_WINDOW), 1) // CHUNK
        for g in range(GM_GROUPS):
            wsm_ref[g] = jnp.where(row_c >= col_c, ws_ref[g], 0.0).astype(BF16)

    pos = pos_ref[0, 0].astype(F32)
    angs = []
    for a in range(tm // LANES):
        col = jnp.broadcast_to(pos[a:a + 1, :], (LANES, LANES)).T
        angs.append(col * inv_ref[...])
    ang = jnp.concatenate(angs, axis=0)
    cos = jnp.cos(ang)
    sin = jnp.sin(ang)
    lane = lax.broadcasted_iota(jnp.int32, (1, LANES), 1)
    lane_lo = (lane % DIFF_HEAD_DIM) < (DIFF_HEAD_DIM // 2)
    sin_signed = jnp.where(lane_lo, -sin, sin)

    x = x_ref[0]
    ms = jnp.mean(x * x, axis=-1, keepdims=True)
    xn = (x * lax.rsqrt(ms + EPS)) * ng_ref[...]
    h = xn * (1.0 + scale_ref[0]) + shift_ref[0]
    hb = h.astype(BF16)

    def proj(sec):
        return jnp.dot(hb, wb_ref[:, sec * SEC:(sec + 1) * SEC], preferred_element_type=F32)

    pv = proj(1)
    pu = proj(0)
    v = _gelu(pv)
    n_win = tm // GM_WINDOW
    for g in range(GM_GROUPS):
        cols = slice(g * LANES, (g + 1) * LANES)
        vg = v[:, cols]
        mu = jnp.mean(vg, axis=-1, keepdims=True)
        vc = vg - mu
        var = jnp.mean(vc * vc, axis=-1, keepdims=True)
        vn = vc * lax.rsqrt(var + EPS)
        vn = vn * lng_ref[:, cols] + lnb_ref[:, cols]
        vln_ref[:, cols] = vn.astype(BF16)
        vwin = jnp.concatenate(
            [vln_ref[w * GM_WINDOW:(w + 1) * GM_WINDOW, cols] for w in range(n_win)], axis=1)
        svg = jnp.dot(wsm_ref[g], vwin, preferred_element_type=F32) + bs_ref[g]
        for w in range(n_win):
            sv_ref[w * GM_WINDOW:(w + 1) * GM_WINDOW, cols] = svg[:, w * LANES:(w + 1) * LANES]
    pz = proj(2)
    gu = _gelu(pu)

    pq = proj(3)
    ya_ref[0] = (gu * sv_ref[...] * _silu(pz)).astype(BF16)
    pk = proj(4)
    qscale = (DIFF_HEAD_DIM ** -0.5) * LOG2E
    for hh, qh in enumerate(_rope_heads(pq, cos, sin_signed, lane_lo)):
        for sub in range(tm // TQ_ATT):
            blk = qh[sub * TQ_ATT:(sub + 1) * TQ_ATT, :] * qscale
            qt_ref[0, hh, sub] = blk.T.astype(BF16)
    pvd = proj(5)
    for hh, kh in enumerate(_rope_heads(pk, cos, sin_signed, lane_lo)):
        k_ref[0, hh] = kh.astype(BF16)
    pzb = proj(6)
    for hh in range(DIFF_HEADS):
        for sub in range(tm // TK_ATT):
            blk = pvd[sub * TK_ATT:(sub + 1) * TK_ATT, hh * LANES:(hh + 1) * LANES]
            vt_ref[0, hh, sub] = blk.T.astype(BF16)
    zb_ref[0] = _silu(pzb).astype(BF16)


def _projection(x, pos4, scale, shift, norm_g, w_in, inv128, ln_g, ln_b, ws, bs3):
    b, s, d = x.shape
    tm = TM_PROJ
    head_shape = jax.ShapeDtypeStruct((b, DIFF_HEADS, s, LANES), BF16)
    row_shape = jax.ShapeDtypeStruct((b, s, SEC), BF16)
    head_spec = pl.BlockSpec((1, DIFF_HEADS, tm, LANES), lambda bi, i: (bi, 0, i, 0))
    row_spec = pl.BlockSpec((1, tm, SEC), lambda bi, i: (bi, i, 0))
    vt_shape = jax.ShapeDtypeStruct((b, DIFF_HEADS, s // TK_ATT, DIFF_V_DIM, TK_ATT), BF16)
    vt_spec = pl.BlockSpec((1, DIFF_HEADS, tm // TK_ATT, DIFF_V_DIM, TK_ATT),
                           lambda bi, i: (bi, 0, i, 0, 0))
    qt_shape = jax.ShapeDtypeStruct((b, DIFF_HEADS, s // TQ_ATT, LANES, TQ_ATT), BF16)
    qt_spec = pl.BlockSpec((1, DIFF_HEADS, tm // TQ_ATT, LANES, TQ_ATT),
                           lambda bi, i: (bi, 0, i, 0, 0))
    const2 = lambda bi, i: (0, 0)
    const3 = lambda bi, i: (0, 0, 0)
    return pl.pallas_call(
        _proj_kernel,
        grid=(b, s // tm),
        in_specs=[
            pl.BlockSpec((1, tm, d), lambda bi, i: (bi, i, 0)),
            pl.BlockSpec((1, 1, tm // LANES, LANES), lambda bi, i: (bi, i, 0, 0)),
            pl.BlockSpec((1, 1, d), lambda bi, i: (bi, 0, 0)),
            pl.BlockSpec((1, 1, d), lambda bi, i: (bi, 0, 0)),
            pl.BlockSpec((1, d), const2),
            pl.BlockSpec((d, N_SEC * SEC), const2,
                         pipeline_mode=pl.Buffered(1)),
            pl.BlockSpec((1, LANES), const2),
            pl.BlockSpec((1, SEC), const2),
            pl.BlockSpec((1, SEC), const2),
            pl.BlockSpec((GM_GROUPS, GM_WINDOW, GM_WINDOW), const3),
            pl.BlockSpec((GM_GROUPS, GM_WINDOW, 1), const3),
        ],
        out_specs=[row_spec, qt_spec, head_spec, vt_spec, row_spec],
        out_shape=[row_shape, qt_shape, head_shape, vt_shape, row_shape],
        scratch_shapes=[pltpu.VMEM((tm, SEC), BF16), pltpu.VMEM((tm, SEC), F32),
                        pltpu.VMEM((d, N_SEC * SEC), BF16),
                        pltpu.VMEM((GM_GROUPS, GM_WINDOW, GM_WINDOW), BF16)],
        compiler_params=pltpu.CompilerParams(
            dimension_semantics=("arbitrary", "arbitrary"), vmem_limit_bytes=VMEM_LIMIT),
        name="in_proj_gmlp_rope",
    )(x, pos4, scale, shift, norm_g, w_in, inv128, ln_g, ln_b, ws, bs3)


def _attn_kernel(lam_init, qt_ref, k_ref, vt_ref, zb_ref, g_ref, lq1_ref, lk1_ref, lq2_ref,
                 lk2_ref, o_ref, q2_ref, acc_ref, m_ref):
    s_len = k_ref.shape[2]
    tq, tk, ng = TQ_ATT, TK_ATT, G_ATT
    n_super = s_len // SUPER_ATT
    tiles_per_super = SUPER_ATT // tk

    lam = (jnp.exp(jnp.sum(lq1_ref[...] * lk1_ref[...], axis=-1, keepdims=True))
           - jnp.exp(jnp.sum(lq2_ref[...] * lk2_ref[...], axis=-1, keepdims=True))
           + lam_init)

    first_feat = lax.broadcasted_iota(jnp.int32, (LANES, 1), 0) < DIFF_HEAD_DIM
    key_pos = lax.broadcasted_iota(jnp.int32, (tk, 2 * tq), 0)
    qry_pos = lax.broadcasted_iota(jnp.int32, (tk, 2 * tq), 1) % tq
    ones_rows = jnp.ones((ONES_ROWS, tk), BF16)

    def allowed(q_off):
        return key_pos // CHUNK <= (qry_pos + q_off) // CHUNK

    def scores(job):
        g, j, _ = job
        kt = k_ref[0, 0, pl.ds(pl.multiple_of(j * tk, tk), tk), :]
        return jnp.dot(kt, q2_ref[g], preferred_element_type=F32)

    def softmax_update(job, s):
        g, _, q_off = job
        if q_off is not None:
            s = jnp.where(allowed(q_off), s, -jnp.inf)
        m_prev = m_ref[g]
        m_new = jnp.maximum(m_prev, jnp.max(s, axis=0, keepdims=True))
        m_ref[g] = m_new
        return jnp.exp2(m_prev - m_new), jnp.exp2(s - m_new).astype(BF16)

    def accumulate(job, alpha, p):
        g, j, _ = job
        vte = jnp.concatenate([vt_ref[0, 0, j], ones_rows], axis=0)
        acc_ref[g] = alpha * acc_ref[g] + jnp.dot(vte, p, preferred_element_type=F32)

    def run_stream(jobs, lookahead, pending_job, state):
        s_cur, alpha, p = state
        pend = (pending_job, alpha, p)
        for i, job in enumerate(jobs):
            nxt = jobs[i + 1] if i + 1 < len(jobs) else lookahead
            s_nxt = scores(nxt) if nxt is not None else None
            alpha, p = softmax_update(job, s_cur)
            accumulate(*pend)
            pend = (job, alpha, p)
            s_cur = s_nxt
        if lookahead is None:
            accumulate(*pend)
            return None
        return s_cur, pend[1], pend[2]

    def load_queries(g, si):
        qt = qt_ref[0, 0, si * ng + g]
        zero = jnp.zeros_like(qt)
        q2_ref[g, :, :tq] = jnp.where(first_feat, qt, zero)
        q2_ref[g, :, tq:] = jnp.where(first_feat, zero, qt)

    def reset_chain(g):
        m_ref[g] = jnp.full(m_ref.shape[1:], -jnp.inf, F32)
        acc_ref[g] = jnp.zeros(acc_ref.shape[1:], F32)

    def finish_chain(g, si):
        acc = acc_ref[g]
        o_all = acc[:DIFF_V_DIM] * (1.0 / acc[DIFF_V_DIM:DIFF_V_DIM + 1])
        o = o_all[:, :tq] - lam * o_all[:, tq:]
        ms = jnp.mean(o * o, axis=0, keepdims=True)
        on = (o * lax.rsqrt(ms + EPS)) * g_ref[...]
        on = (on * (1.0 - lam_init)).T
        rows = pl.ds(pl.multiple_of(si * SUPER_ATT, SUPER_ATT) + g * tq, tq)
        o_ref[0, rows, :] = (on * zb_ref[0, rows, :].astype(F32)).astype(BF16)
        reset_chain(g)

    def full_tiles(jj, state):
        j0 = jj * FULL_TILES_PER_ITER
        jobs = [(g, j0 + u, None) for u in range(FULL_TILES_PER_ITER) for g in range(ng)]
        return run_stream(jobs, (0, j0 + FULL_TILES_PER_ITER, None),
                          (ng - 1, jnp.maximum(j0 - 1, 0), None), state)

    def super_block(si, s_first):
        state = (s_first, jnp.ones((1, 2 * tq), F32), jnp.zeros((tk, 2 * tq), BF16))
        n_full = si * (tiles_per_super // FULL_TILES_PER_ITER)
        state = lax.fori_loop(0, n_full, full_tiles, state)

        j0 = si * tiles_per_super
        jobs = []
        for bt in range(tiles_per_super):
            for g in range(ng):
                q_off = g * tq - bt * tk
                if q_off + tq <= 0:
                    continue
                needs_mask = q_off // CHUNK < (tk - 1) // CHUNK
                jobs.append((g, j0 + bt, q_off if needs_mask else None))
        assert jobs[0][0] == 0 and jobs[-1][0] == ng - 1
        run_stream(jobs, None, (ng - 1, jnp.maximum(j0 - 1, 0), None), state)
        for g in range(ng):
            finish_chain(g, si)
        si_next = jnp.minimum(si + 1, n_super - 1)
        for g in range(ng):
            load_queries(g, si_next)
        return scores((0, 0, None))

    for g0 in range(ng):
        load_queries(g0, 0)
        reset_chain(g0)

    lax.fori_loop(0, n_super, super_block, scores((0, 0, None)))


def _attention(qt, k, vt, zb, subln_g_col, lq1, lk1, lq2, lk2, lam_init):
    b, h, s, _ = k.shape
    head_spec = pl.BlockSpec((1, 1, s, LANES), lambda bi, hi: (bi, hi, 0, 0))
    qt_spec = pl.BlockSpec((1, 1, s // TQ_ATT, LANES, TQ_ATT), lambda bi, hi: (bi, hi, 0, 0, 0))
    vt_spec = pl.BlockSpec((1, 1, s // TK_ATT, DIFF_V_DIM, TK_ATT),
                           lambda bi, hi: (bi, hi, 0, 0, 0))
    col_spec = pl.BlockSpec((1, s, LANES), lambda bi, hi: (bi, 0, hi))
    vec64 = pl.BlockSpec((1, DIFF_HEAD_DIM), lambda bi, hi: (0, 0))
    return pl.pallas_call(
        functools.partial(_attn_kernel, lam_init),
        grid=(b, h),
        in_specs=[qt_spec, head_spec, vt_spec, col_spec,
                  pl.BlockSpec((DIFF_V_DIM, 1), lambda bi, hi: (0, 0)),
                  vec64, vec64, vec64, vec64],
        out_specs=col_spec,
        out_shape=jax.ShapeDtypeStruct((b, s, h * LANES), BF16),
        scratch_shapes=[pltpu.VMEM((G_ATT, LANES, 2 * TQ_ATT), BF16),
                        pltpu.VMEM((G_ATT, DIFF_V_DIM + ONES_ROWS, 2 * TQ_ATT), F32),
                        pltpu.VMEM((G_ATT, 1, 2 * TQ_ATT), F32)],
        compiler_params=pltpu.CompilerParams(
            dimension_semantics=("arbitrary", "arbitrary"), vmem_limit_bytes=VMEM_LIMIT),
        name="diff_attention",
    )(qt, k, vt, zb, subln_g_col, lq1, lk1, lq2, lk2)


def _out_kernel(ya_ref, yb_ref, x_ref, gate_ref, w_ref, fg_ref, o_ref, wb_ref):
    _cast_weight_once(w_ref, wb_ref)
    tm = x_ref.shape[1]
    n_split = tm // OUT_ROW_SPLIT
    ys = []
    for r in range(n_split):
        rows = slice(r * OUT_ROW_SPLIT, (r + 1) * OUT_ROW_SPLIT)
        y = jnp.dot(ya_ref[0, rows, :], wb_ref[:SEC, :], preferred_element_type=F32)
        ys.append(y + jnp.dot(yb_ref[0, rows, :], wb_ref[SEC:, :], preferred_element_type=F32))
    for r in range(n_split):
        rows = slice(r * OUT_ROW_SPLIT, (r + 1) * OUT_ROW_SPLIT)
        xo = x_ref[0, rows, :] + gate_ref[0] * ys[r]
        ms = jnp.mean(xo * xo, axis=-1, keepdims=True)
        o_ref[0, rows, :] = (xo * lax.rsqrt(ms + EPS)) * fg_ref[...]


def _out_projection(ya, yb, x, gate, w_out, final_g):
    b, s, d = x.shape
    tm = TM_OUT
    return pl.pallas_call(
        _out_kernel,
        grid=(b, s // tm),
        in_specs=[
            pl.BlockSpec((1, tm, SEC), lambda bi, i: (bi, i, 0)),
            pl.BlockSpec((1, tm, SEC), lambda bi, i: (bi, i, 0)),
            pl.BlockSpec((1, tm, d), lambda bi, i: (bi, i, 0)),
            pl.BlockSpec((1, 1, d), lambda bi, i: (bi, 0, 0)),
            pl.BlockSpec((2 * SEC, d), lambda bi, i: (0, 0), pipeline_mode=pl.Buffered(1)),
            pl.BlockSpec((1, d), lambda bi, i: (0, 0)),
        ],
        out_specs=pl.BlockSpec((1, tm, d), lambda bi, i: (bi, i, 0)),
        out_shape=jax.ShapeDtypeStruct((b, s, d), F32),
        scratch_shapes=[pltpu.VMEM((2 * SEC, d), BF16)],
        compiler_params=pltpu.CompilerParams(
            dimension_semantics=("arbitrary", "arbitrary"), vmem_limit_bytes=VMEM_LIMIT),
        name="out_proj_residual_norm",
    )(ya, yb, x, gate, w_out, final_g)


def kernel(x, c, positions, w_ada, b_ada, norm_g, w_in, gm_ln_g, gm_ln_b, gm_ws, gm_bs,
           lam_q1, lam_k1, lam_q2, lam_k2, diff_subln_g, w_out, final_g):
    b, s, d = x.shape
    depth = w_ada.shape[0]
    half = DIFF_HEAD_DIM // 2
    inv = ROPE_THETA ** (-jnp.arange(half, dtype=F32) * 2.0 / DIFF_HEAD_DIM)
    inv128 = jnp.tile(inv, LANES // half)[None, :]
    pos4 = positions.reshape(b, s // TM_PROJ, TM_PROJ // LANES, LANES)
    c8 = jnp.zeros((8, d), F32).at[:b].set(c)
    assert depth == 1
    l = 0
    mod = _modulation(c8, w_ada[l], b_ada[l][None, :])[:b]
    shift = mod[:, None, :d]
    scale = mod[:, None, d:2 * d]
    gate = mod[:, None, 2 * d:]
    ya, q, k, vt, zb = _projection(
        x, pos4, scale, shift, norm_g[l][None, :], w_in[l], inv128,
        gm_ln_g[l][None, :], gm_ln_b[l][None, :], gm_ws[l], gm_bs[l][:, :, None])
    lam_init = 0.8 - 0.6 * math.exp(-0.3 * l)
    yb = _attention(q, k, vt, zb, diff_subln_g[l][:, None], lam_q1[l][None, :],
                    lam_k1[l][None, :], lam_q2[l][None, :], lam_k2[l][None, :], lam_init)
    return _out_projection(ya, yb, x, gate, w_out[l], final_g[None, :])
```

```python
import functools
import math

import jax
import jax.numpy as jnp
from jax import lax
from jax.experimental import pallas as pl
from jax.experimental.pallas import tpu as pltpu

F32 = jnp.float32
BF16 = jnp.bfloat16

EPS = 1e-6
CHUNK = 64
GM_GROUPS = 4
GM_GROUP_DIM = 128
GM_WINDOW = 128
DIFF_HEADS = 4
DIFF_HEAD_DIM = 64
DIFF_V_DIM = 128
ROPE_THETA = 10000.0
SEC = 512
N_SEC = 7
LOG2E = 1.4426950408889634

LANES = 128
VMEM_LIMIT = 48 * 1024 * 1024

TM_PROJ = 512
TM_OUT = 1024
OUT_ROW_SPLIT = 256
WEIGHT_CAST_ROWS = 128
TQ_ATT = 256
TK_ATT = 256
G_ATT = 8
SUPER_ATT = TQ_ATT * G_ATT
FULL_TILES_PER_ITER = 8
ONES_ROWS = 16
assert SUPER_ATT % TK_ATT == 0 and TQ_ATT % CHUNK == 0 and TK_ATT % CHUNK == 0
assert (SUPER_ATT // TK_ATT) % FULL_TILES_PER_ITER == 0
assert TM_PROJ % TK_ATT == 0 and TM_PROJ % TQ_ATT == 0


def _silu(x):
    h = 0.5 * x
    return h + h * jnp.tanh(h)


def _gelu(x):
    return 0.5 * x * (1.0 + lax.erf(x * (1.0 / math.sqrt(2.0))))


def _mod_kernel(c_ref, w_ref, b_ref, o_ref):
    cs = _silu(c_ref[...]).astype(BF16)
    w = w_ref[...].astype(BF16)
    o_ref[...] = jnp.dot(cs, w, preferred_element_type=F32) + b_ref[...]


def _modulation(c8, w_ada, b_ada):
    d, n = w_ada.shape
    tn = 1024
    return pl.pallas_call(
        _mod_kernel,
        grid=(n // tn,),
        in_specs=[
            pl.BlockSpec((8, d), lambda j: (0, 0)),
            pl.BlockSpec((d, tn), lambda j: (0, j)),
            pl.BlockSpec((1, tn), lambda j: (0, j)),
        ],
        out_specs=pl.BlockSpec((8, tn), lambda j: (0, j)),
        out_shape=jax.ShapeDtypeStruct((8, n), F32),
        compiler_params=pltpu.CompilerParams(
            dimension_semantics=("arbitrary",), vmem_limit_bytes=VMEM_LIMIT),
        name="adaln_mod",
    )(c8, w_ada, b_ada)


def _cast_weight_once(w_ref, wb_ref):
    first = (pl.program_id(0) == 0) & (pl.program_id(1) == 0)

    @pl.when(first)
    def _():
        def body(i, c):
            rows = pl.ds(pl.multiple_of(i * WEIGHT_CAST_ROWS, WEIGHT_CAST_ROWS), WEIGHT_CAST_ROWS)
            wb_ref[rows, :] = w_ref[rows, :].astype(BF16)
            return c
        lax.fori_loop(0, w_ref.shape[0] // WEIGHT_CAST_ROWS, body, 0)


def _rope_heads(t, cos, sin_signed, lane_lo):
    outs = []
    for hh in range(SEC // LANES):
        xh = t[:, hh * LANES:(hh + 1) * LANES]
        partner = jnp.where(lane_lo, pltpu.roll(xh, LANES - 32, 1), pltpu.roll(xh, 32, 1))
        outs.append(xh * cos + partner * sin_signed)
    return outs


def _proj_kernel(x_ref, pos_ref, scale_ref, shift_ref, ng_ref, w_ref, inv_ref,
                 lng_ref, lnb_ref, ws_ref, bs_ref,
                 ya_ref, qt_ref, k_ref, vt_ref, zb_ref, vln_ref, sv_ref, wb_ref, wsm_ref):
    tm = x_ref.shape[1]
    _cast_weight_once(w_ref, wb_ref)

    @pl.when((pl.program_id(0) == 0) & (pl.program_id(1) == 0))
    def _():
        row_c = lax.broadcasted_iota(jnp.int32, (GM_WINDOW, GM_WINDOW), 0) // CHUNK
        col_c = lax.broadcasted_iota(jnp.int32, (GM_WINDOW, GM_WINDOW), 1) // CHUNK
        for g in range(GM_GROUPS):
            wsm_ref[g] = jnp.where(row_c >= col_c, ws_ref[g], 0.0).astype(BF16)

    pos = pos_ref[0, 0].astype(F32)
    angs = []
    for a in range(tm // LANES):
        col = jnp.broadcast_to(pos[a:a + 1, :], (LANES, LANES)).T
        angs.append(col * inv_ref[...])
    ang = jnp.concatenate(angs, axis=0)
    cos = jnp.cos(ang)
    sin = jnp.sin(ang)
    lane = lax.broadcasted_iota(jnp.int32, (1, LANES), 1)
    lane_lo = (lane % DIFF_HEAD_DIM) < (DIFF_HEAD_DIM // 2)
    sin_signed = jnp.where(lane_lo, -sin, sin)

    x = x_ref[0]
    ms = jnp.mean(x * x, axis=-1, keepdims=True)
    gain = ng_ref[...] * (1.0 + scale_ref[0])
    h = (x * lax.rsqrt(ms + EPS)) * gain + shift_ref[0]
    hb = h.astype(BF16)

    def proj(sec):
        return jnp.dot(hb, wb_ref[:, sec * SEC:(sec + 1) * SEC], preferred_element_type=F32)

    pv = proj(1)
    pu = proj(0)
    v = _gelu(pv)
    n_win = tm // GM_WINDOW
    for g in range(GM_GROUPS):
        cols = slice(g * LANES, (g + 1) * LANES)
        vg = v[:, cols]
        mu = jnp.mean(vg, axis=-1, keepdims=True)
        vc = vg - mu
        var = jnp.mean(vc * vc, axis=-1, keepdims=True)
        vn = vc * lax.rsqrt(var + EPS)
        vn = vn * lng_ref[:, cols] + lnb_ref[:, cols]
        vln_ref[:, cols] = vn.astype(BF16)
        vwin = jnp.concatenate(
            [vln_ref[w * GM_WINDOW:(w + 1) * GM_WINDOW, cols] for w in range(n_win)], axis=1)
        svg = jnp.dot(wsm_ref[g], vwin, preferred_element_type=F32) + bs_ref[g]
        for w in range(n_win):
            sv_ref[w * GM_WINDOW:(w + 1) * GM_WINDOW, cols] = svg[:, w * LANES:(w + 1) * LANES]
    pz = proj(2)
    gu = _gelu(pu)

    pq = proj(3)
    ya_ref[0] = (gu * sv_ref[...] * _silu(pz)).astype(BF16)
    pk = proj(4)
    qscale = (DIFF_HEAD_DIM ** -0.5) * LOG2E
    for hh, qh in enumerate(_rope_heads(pq, cos, sin_signed, lane_lo)):
        for sub in range(tm // TQ_ATT):
            blk = qh[sub * TQ_ATT:(sub + 1) * TQ_ATT, :] * qscale
            qt_ref[0, hh, sub] = blk.T.astype(BF16)
    pvd = proj(5)
    for hh, kh in enumerate(_rope_heads(pk, cos, sin_signed, lane_lo)):
        k_ref[0, hh] = kh.astype(BF16)
    pzb = proj(6)
    for hh in range(DIFF_HEADS):
        for sub in range(tm // TK_ATT):
            blk = pvd[sub * TK_ATT:(sub + 1) * TK_ATT, hh * LANES:(hh + 1) * LANES]
            vt_ref[0, hh, sub] = blk.T.astype(BF16)
    zb_ref[0] = _silu(pzb).astype(BF16)


def _projection(x, pos4, scale, shift, norm_g, w_in, inv128, ln_g, ln_b, ws, bs3):
    b, s, d = x.shape
    tm = TM_PROJ
    head_shape = jax.ShapeDtypeStruct((b, DIFF_HEADS, s, LANES), BF16)
    row_shape = jax.ShapeDtypeStruct((b, s, SEC), BF16)
    head_spec = pl.BlockSpec((1, DIFF_HEADS, tm, LANES), lambda bi, i: (bi, 0, i, 0))
    row_spec = pl.BlockSpec((1, tm, SEC), lambda bi, i: (bi, i, 0))
    vt_shape = jax.ShapeDtypeStruct((b, DIFF_HEADS, s // TK_ATT, DIFF_V_DIM, TK_ATT), BF16)
    vt_spec = pl.BlockSpec((1, DIFF_HEADS, tm // TK_ATT, DIFF_V_DIM, TK_ATT),
                           lambda bi, i: (bi, 0, i, 0, 0))
    qt_shape = jax.ShapeDtypeStruct((b, DIFF_HEADS, s // TQ_ATT, LANES, TQ_ATT), BF16)
    qt_spec = pl.BlockSpec((1, DIFF_HEADS, tm // TQ_ATT, LANES, TQ_ATT),
                           lambda bi, i: (bi, 0, i, 0, 0))
    const2 = lambda bi, i: (0, 0)
    const3 = lambda bi, i: (0, 0, 0)
    return pl.pallas_call(
        _proj_kernel,
        grid=(b, s // tm),
        in_specs=[
            pl.BlockSpec((1, tm, d), lambda bi, i: (bi, i, 0)),
            pl.BlockSpec((1, 1, tm // LANES, LANES), lambda bi, i: (bi, i, 0, 0)),
            pl.BlockSpec((1, 1, d), lambda bi, i: (bi, 0, 0)),
            pl.BlockSpec((1, 1, d), lambda bi, i: (bi, 0, 0)),
            pl.BlockSpec((1, d), const2),
            pl.BlockSpec((d, N_SEC * SEC), const2,
                         pipeline_mode=pl.Buffered(1)),
            pl.BlockSpec((1, LANES), const2),
            pl.BlockSpec((1, SEC), const2),
            pl.BlockSpec((1, SEC), const2),
            pl.BlockSpec((GM_GROUPS, GM_WINDOW, GM_WINDOW), const3),
            pl.BlockSpec((GM_GROUPS, GM_WINDOW, 1), const3),
        ],
        out_specs=[row_spec, qt_spec, head_spec, vt_spec, row_spec],
        out_shape=[row_shape, qt_shape, head_shape, vt_shape, row_shape],
        scratch_shapes=[pltpu.VMEM((tm, SEC), BF16), pltpu.VMEM((tm, SEC), F32),
                        pltpu.VMEM((d, N_SEC * SEC), BF16),
                        pltpu.VMEM((GM_GROUPS, GM_WINDOW, GM_WINDOW), BF16)],
        compiler_params=pltpu.CompilerParams(
            dimension_semantics=("arbitrary", "arbitrary"), vmem_limit_bytes=VMEM_LIMIT),
        name="in_proj_gmlp_rope",
    )(x, pos4, scale, shift, norm_g, w_in, inv128, ln_g, ln_b, ws, bs3)


def _attn_kernel(lam_init, qt_ref, k_ref, vt_ref, zb_ref, g_ref, lq1_ref, lk1_ref, lq2_ref,
                 lk2_ref, o_ref, q2_ref, acc_ref, m_ref):
    s_len = k_ref.shape[2]
    tq, tk, ng = TQ_ATT, TK_ATT, G_ATT
    n_super = s_len // SUPER_ATT
    tiles_per_super = SUPER_ATT // tk

    lam = (jnp.exp(jnp.sum(lq1_ref[...] * lk1_ref[...], axis=-1, keepdims=True))
           - jnp.exp(jnp.sum(lq2_ref[...] * lk2_ref[...], axis=-1, keepdims=True))
           + lam_init)

    norm_gain = g_ref[...] * (1.0 - lam_init)
    first_feat = lax.broadcasted_iota(jnp.int32, (LANES, 1), 0) < DIFF_HEAD_DIM
    key_pos = lax.broadcasted_iota(jnp.int32, (tk, 2 * tq), 0)
    qry_pos = lax.broadcasted_iota(jnp.int32, (tk, 2 * tq), 1) % tq
    ones_rows = jnp.ones((ONES_ROWS, tk), BF16)

    def allowed(q_off):
        return key_pos // CHUNK <= (qry_pos + q_off) // CHUNK

    def scores(job):
        g, j, _ = job
        kt = k_ref[0, 0, pl.ds(pl.multiple_of(j * tk, tk), tk), :]
        return jnp.dot(kt, q2_ref[g], preferred_element_type=F32)

    def softmax_update(job, s):
        g, _, q_off = job
        if q_off is not None:
            s = jnp.where(allowed(q_off), s, -jnp.inf)
        m_prev = m_ref[g]
        m_new = jnp.maximum(m_prev, jnp.max(s, axis=0, keepdims=True))
        m_ref[g] = m_new
        return jnp.exp2(m_prev - m_new), jnp.exp2(s - m_new).astype(BF16)

    def accumulate(job, alpha, p):
        g, j, _ = job
        vte = jnp.concatenate([vt_ref[0, 0, j], ones_rows], axis=0)
        acc_ref[g] = alpha * acc_ref[g] + jnp.dot(vte, p, preferred_element_type=F32)

    def run_stream(jobs, lookahead, pending_job, state):
        s_cur, alpha, p = state
        pend = (pending_job, alpha, p)
        for i, job in enumerate(jobs):
            nxt = jobs[i + 1] if i + 1 < len(jobs) else lookahead
            s_nxt = scores(nxt) if nxt is not None else None
            alpha, p = softmax_update(job, s_cur)
            accumulate(*pend)
            pend = (job, alpha, p)
            s_cur = s_nxt
        if lookahead is None:
            accumulate(*pend)
            return None
        return s_cur, pend[1], pend[2]

    def load_queries(g, si):
        qt = qt_ref[0, 0, si * ng + g]
        zero = jnp.zeros_like(qt)
        q2_ref[g, :, :tq] = jnp.where(first_feat, qt, zero)
        q2_ref[g, :, tq:] = jnp.where(first_feat, zero, qt)

    def reset_chain(g):
        m_ref[g] = jnp.full(m_ref.shape[1:], -jnp.inf, F32)
        acc_ref[g] = jnp.zeros(acc_ref.shape[1:], F32)

    def finish_chain(g, si):
        acc = acc_ref[g]
        o_all = acc[:DIFF_V_DIM] * (1.0 / acc[DIFF_V_DIM:DIFF_V_DIM + 1])
        o = o_all[:, :tq] - lam * o_all[:, tq:]
        ms = jnp.mean(o * o, axis=0, keepdims=True)
        on = ((o * lax.rsqrt(ms + EPS)) * norm_gain).T
        rows = pl.ds(pl.multiple_of(si * SUPER_ATT, SUPER_ATT) + g * tq, tq)
        o_ref[0, rows, :] = (on * zb_ref[0, rows, :].astype(F32)).astype(BF16)
        reset_chain(g)

    def full_tiles(jj, state):
        j0 = jj * FULL_TILES_PER_ITER
        jobs = [(g, j0 + u, None) for u in range(FULL_TILES_PER_ITER) for g in range(ng)]
        return run_stream(jobs, (0, j0 + FULL_TILES_PER_ITER, None),
                          (ng - 1, jnp.maximum(j0 - 1, 0), None), state)

    def super_block(si, carry):
        for g in range(ng):
            load_queries(g, si)
        s_first = scores((0, 0, None))
        state = (s_first, jnp.ones((1, 2 * tq), F32), jnp.zeros((tk, 2 * tq), BF16))
        n_full = si * (tiles_per_super // FULL_TILES_PER_ITER)
        state = lax.fori_loop(0, n_full, full_tiles, state)

        j0 = si * tiles_per_super
        jobs = []
        for bt in range(tiles_per_super):
            for g in range(ng):
                q_off = g * tq - bt * tk
                if q_off + tq <= 0:
                    continue
                needs_mask = q_off // CHUNK < (tk - 1) // CHUNK
                jobs.append((g, j0 + bt, q_off if needs_mask else None))
        assert jobs[0][0] == 0 and jobs[-1][0] == ng - 1
        run_stream(jobs, None, (ng - 1, jnp.maximum(j0 - 1, 0), None), state)
        for g in range(ng):
            finish_chain(g, si)
        return carry

    for g0 in range(ng):
        reset_chain(g0)

    lax.fori_loop(0, n_super, super_block, 0)


def _attention(qt, k, vt, zb, subln_g_col, lq1, lk1, lq2, lk2, lam_init):
    b, h, s, _ = k.shape
    head_spec = pl.BlockSpec((1, 1, s, LANES), lambda bi, hi: (bi, hi, 0, 0))
    qt_spec = pl.BlockSpec((1, 1, s // TQ_ATT, LANES, TQ_ATT), lambda bi, hi: (bi, hi, 0, 0, 0))
    vt_spec = pl.BlockSpec((1, 1, s // TK_ATT, DIFF_V_DIM, TK_ATT),
                           lambda bi, hi: (bi, hi, 0, 0, 0))
    col_spec = pl.BlockSpec((1, s, LANES), lambda bi, hi: (bi, 0, hi))
    vec64 = pl.BlockSpec((1, DIFF_HEAD_DIM), lambda bi, hi: (0, 0))
    return pl.pallas_call(
        functools.partial(_attn_kernel, lam_init),
        grid=(b, h),
        in_specs=[qt_spec, head_spec, vt_spec, col_spec,
                  pl.BlockSpec((DIFF_V_DIM, 1), lambda bi, hi: (0, 0)),
                  vec64, vec64, vec64, vec64],
        out_specs=col_spec,
        out_shape=jax.ShapeDtypeStruct((b, s, h * LANES), BF16),
        scratch_shapes=[pltpu.VMEM((G_ATT, LANES, 2 * TQ_ATT), BF16),
                        pltpu.VMEM((G_ATT, DIFF_V_DIM + ONES_ROWS, 2 * TQ_ATT), F32),
                        pltpu.VMEM((G_ATT, 1, 2 * TQ_ATT), F32)],
        compiler_params=pltpu.CompilerParams(
            dimension_semantics=("arbitrary", "arbitrary"), vmem_limit_bytes=VMEM_LIMIT),
        name="diff_attention",
    )(qt, k, vt, zb, subln_g_col, lq1, lk1, lq2, lk2)


def _out_kernel(ya_ref, yb_ref, x_ref, gate_ref, w_ref, fg_ref, o_ref, wb_ref):
    _cast_weight_once(w_ref, wb_ref)
    tm = x_ref.shape[1]
    n_split = tm // OUT_ROW_SPLIT
    ys = []
    for r in range(n_split):
        rows = slice(r * OUT_ROW_SPLIT, (r + 1) * OUT_ROW_SPLIT)
        y = jnp.dot(ya_ref[0, rows, :], wb_ref[:SEC, :], preferred_element_type=F32)
        ys.append(y + jnp.dot(yb_ref[0, rows, :], wb_ref[SEC:, :], preferred_element_type=F32))
    for r in range(n_split):
        rows = slice(r * OUT_ROW_SPLIT, (r + 1) * OUT_ROW_SPLIT)
        xo = x_ref[0, rows, :] + gate_ref[0] * ys[r]
        ms = jnp.mean(xo * xo, axis=-1, keepdims=True)
        o_ref[0, rows, :] = (xo * lax.rsqrt(ms + EPS)) * fg_ref[...]


def _out_projection(ya, yb, x, gate, w_out, final_g):
    b, s, d = x.shape
    tm = TM_OUT
    return pl.pallas_call(
        _out_kernel,
        grid=(b, s // tm),
        in_specs=[
            pl.BlockSpec((1, tm, SEC), lambda bi, i: (bi, i, 0)),
            pl.BlockSpec((1, tm, SEC), lambda bi, i: (bi, i, 0)),
            pl.BlockSpec((1, tm, d), lambda bi, i: (bi, i, 0)),
            pl.BlockSpec((1, 1, d), lambda bi, i: (bi, 0, 0)),
            pl.BlockSpec((2 * SEC, d), lambda bi, i: (0, 0), pipeline_mode=pl.Buffered(1)),
            pl.BlockSpec((1, d), lambda bi, i: (0, 0)),
        ],
        out_specs=pl.BlockSpec((1, tm, d), lambda bi, i: (bi, i, 0)),
        out_shape=jax.ShapeDtypeStruct((b, s, d), F32),
        scratch_shapes=[pltpu.VMEM((2 * SEC, d), BF16)],
        compiler_params=pltpu.CompilerParams(
            dimension_semantics=("arbitrary", "arbitrary"), vmem_limit_bytes=VMEM_LIMIT),
        name="out_proj_residual_norm",
    )(ya, yb, x, gate, w_out, final_g)


def kernel(x, c, positions, w_ada, b_ada, norm_g, w_in, gm_ln_g, gm_ln_b, gm_ws, gm_bs,
           lam_q1, lam_k1, lam_q2, lam_k2, diff_subln_g, w_out, final_g):
    b, s, d = x.shape
    depth = w_ada.shape[0]
    half = DIFF_HEAD_DIM // 2
    inv = ROPE_THETA ** (-jnp.arange(half, dtype=F32) * 2.0 / DIFF_HEAD_DIM)
    inv128 = jnp.tile(inv, LANES // half)[None, :]
    pos4 = positions.reshape(b, s // TM_PROJ, TM_PROJ // LANES, LANES)
    c8 = jnp.zeros((8, d), F32).at[:b].set(c)
    assert depth == 1
    l = 0
    mod = _modulation(c8, w_ada[l], b_ada[l][None, :])[:b]
    shift = mod[:, None, :d]
    scale = mod[:, None, d:2 * d]
    gate = mod[:, None, 2 * d:]
    ya, q, k, vt, zb = _projection(
        x, pos4, scale, shift, norm_g[l][None, :], w_in[l], inv128,
        gm_ln_g[l][None, :], gm_ln_b[l][None, :], gm_ws[l], gm_bs[l][:, :, None])
    lam_init = 0.8 - 0.6 * math.exp(-0.3 * l)
    yb = _attention(q, k, vt, zb, diff_subln_g[l][:, None], lam_q1[l][None, :],
                    lam_k1[l][None, :], lam_q2[l][None, :], lam_k2[l][None, :], lam_init)
    return _out_projection(ya, yb, x, gate, w_out[l], final_g[None, :])
```

```python
import functools
import math

import jax
import jax.numpy as jnp
from jax import lax
from jax.experimental import pallas as pl
from jax.experimental.pallas import tpu as pltpu

F32 = jnp.float32
BF16 = jnp.bfloat16

EPS = 1e-6
CHUNK = 64
GM_GROUPS = 4
GM_GROUP_DIM = 128
GM_WINDOW = 128
DIFF_HEADS = 4
DIFF_HEAD_DIM = 64
DIFF_V_DIM = 128
ROPE_THETA = 10000.0
SEC = 512
N_SEC = 7
LOG2E = 1.4426950408889634

LANES = 128
VMEM_LIMIT = 48 * 1024 * 1024

TM_PROJ = 512
TM_OUT = 1024
OUT_ROW_SPLIT = 256
WEIGHT_CAST_ROWS = 128
TQ_ATT = 256
TK_ATT = 256
G_ATT = 8
SUPER_ATT = TQ_ATT * G_ATT
FULL_TILES_PER_ITER = 8
ONES_ROWS = 16
assert SUPER_ATT % TK_ATT == 0 and TQ_ATT % CHUNK == 0 and TK_ATT % CHUNK == 0
assert (SUPER_ATT // TK_ATT) % FULL_TILES_PER_ITER == 0
assert TM_PROJ % TK_ATT == 0 and TM_PROJ % TQ_ATT == 0


def _silu(x):
    h = 0.5 * x
    return h + h * jnp.tanh(h)


def _gelu(x):
    return 0.5 * x * (1.0 + lax.erf(x * (1.0 / math.sqrt(2.0))))


def _mod_kernel(c_ref, w_ref, b_ref, o_ref):
    cs = _silu(c_ref[...]).astype(BF16)
    w = w_ref[...].astype(BF16)
    o_ref[...] = jnp.dot(cs, w, preferred_element_type=F32) + b_ref[...]


def _modulation(c8, w_ada, b_ada):
    d, n = w_ada.shape
    tn = 1024
    return pl.pallas_call(
        _mod_kernel,
        grid=(n // tn,),
        in_specs=[
            pl.BlockSpec((8, d), lambda j: (0, 0)),
            pl.BlockSpec((d, tn), lambda j: (0, j)),
            pl.BlockSpec((1, tn), lambda j: (0, j)),
        ],
        out_specs=pl.BlockSpec((8, tn), lambda j: (0, j)),
        out_shape=jax.ShapeDtypeStruct((8, n), F32),
        compiler_params=pltpu.CompilerParams(
            dimension_semantics=("arbitrary",), vmem_limit_bytes=VMEM_LIMIT),
        name="adaln_mod",
    )(c8, w_ada, b_ada)


def _cast_weight_once(w_ref, wb_ref):
    first = (pl.program_id(0) == 0) & (pl.program_id(1) == 0)

    @pl.when(first)
    def _():
        def body(i, c):
            rows = pl.ds(pl.multiple_of(i * WEIGHT_CAST_ROWS, WEIGHT_CAST_ROWS), WEIGHT_CAST_ROWS)
            wb_ref[rows, :] = w_ref[rows, :].astype(BF16)
            return c
        lax.fori_loop(0, w_ref.shape[0] // WEIGHT_CAST_ROWS, body, 0)


def _rope_heads(t, cos, sin_signed, lane_lo):
    outs = []
    for hh in range(SEC // LANES):
        xh = t[:, hh * LANES:(hh + 1) * LANES]
        partner = jnp.where(lane_lo, pltpu.roll(xh, LANES - 32, 1), pltpu.roll(xh, 32, 1))
        outs.append(xh * cos + partner * sin_signed)
    return outs


def _proj_kernel(x_ref, pos_ref, scale_ref, shift_ref, ng_ref, w_ref, inv_ref,
                 lng_ref, lnb_ref, ws_ref, bs_ref,
                 ya_ref, qt_ref, k_ref, vt_ref, zb_ref, vln_ref, sv_ref, wb_ref, wsm_ref):
    tm = x_ref.shape[1]
    _cast_weight_once(w_ref, wb_ref)

    @pl.when((pl.program_id(0) == 0) & (pl.program_id(1) == 0))
    def _():
        row_c = lax.broadcasted_iota(jnp.int32, (GM_WINDOW, GM_WINDOW), 0) // CHUNK
        col_c = lax.broadcasted_iota(jnp.int32, (GM_WINDOW, GM_WINDOW), 1) // CHUNK
        for g in range(GM_GROUPS):
            wsm_ref[g] = jnp.where(row_c >= col_c, ws_ref[g], 0.0).astype(BF16)

    pos = pos_ref[0, 0].astype(F32)
    angs = []
    for a in range(tm // LANES):
        col = jnp.broadcast_to(pos[a:a + 1, :], (LANES, LANES)).T
        angs.append(col * inv_ref[...])
    ang = jnp.concatenate(angs, axis=0)
    cos = jnp.cos(ang)
    sin = jnp.sin(ang)
    lane = lax.broadcasted_iota(jnp.int32, (1, LANES), 1)
    lane_lo = (lane % DIFF_HEAD_DIM) < (DIFF_HEAD_DIM // 2)
    sin_signed = jnp.where(lane_lo, -sin, sin)

    x = x_ref[0]
    ms = jnp.mean(x * x, axis=-1, keepdims=True)
    gain = ng_ref[...] * (1.0 + scale_ref[0])
    h = (x * lax.rsqrt(ms + EPS)) * gain + shift_ref[0]
    hb = h.astype(BF16)

    def proj(sec):
        return jnp.dot(hb, wb_ref[:, sec * SEC:(sec + 1) * SEC], preferred_element_type=F32)

    pv = proj(1)
    pu = proj(0)
    v = _gelu(pv)
    n_win = tm // GM_WINDOW
    for g in range(GM_GROUPS):
        cols = slice(g * LANES, (g + 1) * LANES)
        vg = v[:, cols]
        mu = jnp.mean(vg, axis=-1, keepdims=True)
        vc = vg - mu
        var = jnp.mean(vc * vc, axis=-1, keepdims=True)
        vn = vc * lax.rsqrt(var + EPS)
        vn = vn * lng_ref[:, cols] + lnb_ref[:, cols]
        vln_ref[:, cols] = vn.astype(BF16)
        vwin = jnp.concatenate(
            [vln_ref[w * GM_WINDOW:(w + 1) * GM_WINDOW, cols] for w in range(n_win)], axis=1)
        svg = jnp.dot(wsm_ref[g], vwin, preferred_element_type=F32) + bs_ref[g]
        for w in range(n_win):
            sv_ref[w * GM_WINDOW:(w + 1) * GM_WINDOW, cols] = svg[:, w * LANES:(w + 1) * LANES]
    pz = proj(2)
    gu = _gelu(pu)

    pq = proj(3)
    ya_ref[0] = (gu * sv_ref[...] * _silu(pz)).astype(BF16)
    pk = proj(4)
    qscale = (DIFF_HEAD_DIM ** -0.5) * LOG2E
    for hh, qh in enumerate(_rope_heads(pq, cos, sin_signed, lane_lo)):
        for sub in range(tm // TQ_ATT):
            blk = qh[sub * TQ_ATT:(sub + 1) * TQ_ATT, :] * qscale
            qt_ref[0, hh, sub] = blk.T.astype(BF16)
    pvd = proj(5)
    for hh, kh in enumerate(_rope_heads(pk, cos, sin_signed, lane_lo)):
        k_ref[0, hh] = kh.astype(BF16)
    pzb = proj(6)
    for hh in range(DIFF_HEADS):
        for sub in range(tm // TK_ATT):
            blk = pvd[sub * TK_ATT:(sub + 1) * TK_ATT, hh * LANES:(hh + 1) * LANES]
            vt_ref[0, hh, sub] = blk.T.astype(BF16)
    zb_ref[0] = _silu(pzb).astype(BF16)


def _projection(x, pos4, scale, shift, norm_g, w_in, inv128, ln_g, ln_b, ws, bs3):
    b, s, d = x.shape
    tm = TM_PROJ
    head_shape = jax.ShapeDtypeStruct((b, DIFF_HEADS, s, LANES), BF16)
    row_shape = jax.ShapeDtypeStruct((b, s, SEC), BF16)
    head_spec = pl.BlockSpec((1, DIFF_HEADS, tm, LANES), lambda bi, i: (bi, 0, i, 0))
    row_spec = pl.BlockSpec((1, tm, SEC), lambda bi, i: (bi, i, 0))
    vt_shape = jax.ShapeDtypeStruct((b, DIFF_HEADS, s // TK_ATT, DIFF_V_DIM, TK_ATT), BF16)
    vt_spec = pl.BlockSpec((1, DIFF_HEADS, tm // TK_ATT, DIFF_V_DIM, TK_ATT),
                           lambda bi, i: (bi, 0, i, 0, 0))
    qt_shape = jax.ShapeDtypeStruct((b, DIFF_HEADS, s // TQ_ATT, LANES, TQ_ATT), BF16)
    qt_spec = pl.BlockSpec((1, DIFF_HEADS, tm // TQ_ATT, LANES, TQ_ATT),
                           lambda bi, i: (bi, 0, i, 0, 0))
    const2 = lambda bi, i: (0, 0)
    const3 = lambda bi, i: (0, 0, 0)
    return pl.pallas_call(
        _proj_kernel,
        grid=(b, s // tm),
        in_specs=[
            pl.BlockSpec((1, tm, d), lambda bi, i: (bi, i, 0)),
            pl.BlockSpec((1, 1, tm // LANES, LANES), lambda bi, i: (bi, i, 0, 0)),
            pl.BlockSpec((1, 1, d), lambda bi, i: (bi, 0, 0)),
            pl.BlockSpec((1, 1, d), lambda bi, i: (bi, 0, 0)),
            pl.BlockSpec((1, d), const2),
            pl.BlockSpec((d, N_SEC * SEC), const2,
                         pipeline_mode=pl.Buffered(1)),
            pl.BlockSpec((1, LANES), const2),
            pl.BlockSpec((1, SEC), const2),
            pl.BlockSpec((1, SEC), const2),
            pl.BlockSpec((GM_GROUPS, GM_WINDOW, GM_WINDOW), const3),
            pl.BlockSpec((GM_GROUPS, GM_WINDOW, 1), const3),
        ],
        out_specs=[row_spec, qt_spec, head_spec, vt_spec, row_spec],
        out_shape=[row_shape, qt_shape, head_shape, vt_shape, row_shape],
        scratch_shapes=[pltpu.VMEM((tm, SEC), BF16), pltpu.VMEM((tm, SEC), F32),
                        pltpu.VMEM((d, N_SEC * SEC), BF16),
                        pltpu.VMEM((GM_GROUPS, GM_WINDOW, GM_WINDOW), BF16)],
        compiler_params=pltpu.CompilerParams(
            dimension_semantics=("arbitrary", "arbitrary"), vmem_limit_bytes=VMEM_LIMIT),
        name="in_proj_gmlp_rope",
    )(x, pos4, scale, shift, norm_g, w_in, inv128, ln_g, ln_b, ws, bs3)


def _attn_kernel(lam_init, qt_ref, k_ref, vt_ref, zb_ref, g_ref, lq1_ref, lk1_ref, lq2_ref,
                 lk2_ref, o_ref, q2_ref, acc_ref, m_ref):
    s_len = k_ref.shape[2]
    tq, tk, ng = TQ_ATT, TK_ATT, G_ATT
    n_super = s_len // SUPER_ATT
    tiles_per_super = SUPER_ATT // tk

    lam = (jnp.exp(jnp.sum(lq1_ref[...] * lk1_ref[...], axis=-1, keepdims=True))
           - jnp.exp(jnp.sum(lq2_ref[...] * lk2_ref[...], axis=-1, keepdims=True))
           + lam_init)

    norm_gain = g_ref[...] * (1.0 - lam_init)
    first_feat = lax.broadcasted_iota(jnp.int32, (LANES, 1), 0) < DIFF_HEAD_DIM
    key_pos = lax.broadcasted_iota(jnp.int32, (tk, 2 * tq), 0)
    qry_pos = lax.broadcasted_iota(jnp.int32, (tk, 2 * tq), 1) % tq
    ones_rows = jnp.ones((ONES_ROWS, tk), BF16)

    def allowed(q_off):
        return key_pos // CHUNK <= (qry_pos + q_off) // CHUNK

    def scores(job):
        g, j, _ = job
        kt = k_ref[0, 0, pl.ds(pl.multiple_of(j * tk, tk), tk), :]
        return jnp.dot(kt, q2_ref[g], preferred_element_type=F32)

    def softmax_update(job, s):
        g, _, q_off = job
        if q_off is not None:
            s = jnp.where(allowed(q_off), s, -jnp.inf)
        m_prev = m_ref[g]
        m_new = jnp.maximum(m_prev, jnp.max(s, axis=0, keepdims=True))
        m_ref[g] = m_new
        return jnp.exp2(m_prev - m_new), jnp.exp2(s - m_new).astype(BF16)

    def accumulate(job, alpha, p):
        g, j, _ = job
        vte = jnp.concatenate([vt_ref[0, 0, j], ones_rows], axis=0)
        acc_ref[g] = alpha * acc_ref[g] + jnp.dot(vte, p, preferred_element_type=F32)

    def run_stream(jobs, lookahead, pending_job, state):
        s_cur, alpha, p = state
        pend = (pending_job, alpha, p)
        for i, job in enumerate(jobs):
            nxt = jobs[i + 1] if i + 1 < len(jobs) else lookahead
            s_nxt = scores(nxt) if nxt is not None else None
            alpha, p = softmax_update(job, s_cur)
            accumulate(*pend)
            pend = (job, alpha, p)
            s_cur = s_nxt
        if lookahead is None:
            accumulate(*pend)
            return None
        return s_cur, pend[1], pend[2]

    def load_queries(g, si):
        qt = qt_ref[0, 0, si * ng + g]
        zero = jnp.zeros_like(qt)
        q2_ref[g, :, :tq] = jnp.where(first_feat, qt, zero)
        q2_ref[g, :, tq:] = jnp.where(first_feat, zero, qt)

    def reset_chain(g):
        m_ref[g] = jnp.full(m_ref.shape[1:], -jnp.inf, F32)
        acc_ref[g] = jnp.zeros(acc_ref.shape[1:], F32)

    def finish_chain(g, si):
        acc = acc_ref[g]
        o_all = acc[:DIFF_V_DIM] * (1.0 / acc[DIFF_V_DIM:DIFF_V_DIM + 1])
        o = o_all[:, :tq] - lam * o_all[:, tq:]
        ms = jnp.mean(o * o, axis=0, keepdims=True)
        on = ((o * lax.rsqrt(ms + EPS)) * norm_gain).T
        rows = pl.ds(pl.multiple_of(si * SUPER_ATT, SUPER_ATT) + g * tq, tq)
        o_ref[0, rows, :] = (on * zb_ref[0, rows, :].astype(F32)).astype(BF16)

    def full_tiles(jj, state):
        j0 = jj * FULL_TILES_PER_ITER
        jobs = [(g, j0 + u, None) for u in range(FULL_TILES_PER_ITER) for g in range(ng)]
        return run_stream(jobs, (0, j0 + FULL_TILES_PER_ITER, None),
                          (ng - 1, jnp.maximum(j0 - 1, 0), None), state)

    def super_block(si, carry):
        for g in range(ng):
            load_queries(g, si)
            reset_chain(g)
        s_first = scores((0, 0, None))
        state = (s_first, jnp.ones((1, 2 * tq), F32), jnp.zeros((tk, 2 * tq), BF16))
        n_full = si * (tiles_per_super // FULL_TILES_PER_ITER)
        state = lax.fori_loop(0, n_full, full_tiles, state)

        j0 = si * tiles_per_super
        jobs = []
        for bt in range(tiles_per_super):
            for g in range(ng):
                q_off = g * tq - bt * tk
                if q_off + tq <= 0:
                    continue
                needs_mask = q_off // CHUNK < (tk - 1) // CHUNK
                jobs.append((g, j0 + bt, q_off if needs_mask else None))
        assert jobs[0][0] == 0 and jobs[-1][0] == ng - 1
        run_stream(jobs, None, (ng - 1, jnp.maximum(j0 - 1, 0), None), state)
        for g in range(ng):
            finish_chain(g, si)
        return carry

    lax.fori_loop(0, n_super, super_block, 0)


def _attention(qt, k, vt, zb, subln_g_col, lq1, lk1, lq2, lk2, lam_init):
    b, h, s, _ = k.shape
    head_spec = pl.BlockSpec((1, 1, s, LANES), lambda bi, hi: (bi, hi, 0, 0))
    qt_spec = pl.BlockSpec((1, 1, s // TQ_ATT, LANES, TQ_ATT), lambda bi, hi: (bi, hi, 0, 0, 0))
    vt_spec = pl.BlockSpec((1, 1, s // TK_ATT, DIFF_V_DIM, TK_ATT),
                           lambda bi, hi: (bi, hi, 0, 0, 0))
    col_spec = pl.BlockSpec((1, s, LANES), lambda bi, hi: (bi, 0, hi))
    vec64 = pl.BlockSpec((1, DIFF_HEAD_DIM), lambda bi, hi: (0, 0))
    return pl.pallas_call(
        functools.partial(_attn_kernel, lam_init),
        grid=(b, h),
        in_specs=[qt_spec, head_spec, vt_spec, col_spec,
                  pl.BlockSpec((DIFF_V_DIM, 1), lambda bi, hi: (0, 0)),
                  vec64, vec64, vec64, vec64],
        out_specs=col_spec,
        out_shape=jax.ShapeDtypeStruct((b, s, h * LANES), BF16),
        scratch_shapes=[pltpu.VMEM((G_ATT, LANES, 2 * TQ_ATT), BF16),
                        pltpu.VMEM((G_ATT, DIFF_V_DIM + ONES_ROWS, 2 * TQ_ATT), F32),
                        pltpu.VMEM((G_ATT, 1, 2 * TQ_ATT), F32)],
        compiler_params=pltpu.CompilerParams(
            dimension_semantics=("arbitrary", "arbitrary"), vmem_limit_bytes=VMEM_LIMIT),
        name="diff_attention",
    )(qt, k, vt, zb, subln_g_col, lq1, lk1, lq2, lk2)


def _out_kernel(ya_ref, yb_ref, x_ref, gate_ref, w_ref, fg_ref, o_ref, wb_ref):
    _cast_weight_once(w_ref, wb_ref)
    tm = x_ref.shape[1]
    n_split = tm // OUT_ROW_SPLIT
    ys = []
    for r in range(n_split):
        rows = slice(r * OUT_ROW_SPLIT, (r + 1) * OUT_ROW_SPLIT)
        y = jnp.dot(ya_ref[0, rows, :], wb_ref[:SEC, :], preferred_element_type=F32)
        ys.append(y + jnp.dot(yb_ref[0, rows, :], wb_ref[SEC:, :], preferred_element_type=F32))
    for r in range(n_split):
        rows = slice(r * OUT_ROW_SPLIT, (r + 1) * OUT_ROW_SPLIT)
        xo = x_ref[0, rows, :] + gate_ref[0] * ys[r]
        ms = jnp.mean(xo * xo, axis=-1, keepdims=True)
        o_ref[0, rows, :] = (xo * lax.rsqrt(ms + EPS)) * fg_ref[...]


def _out_projection(ya, yb, x, gate, w_out, final_g):
    b, s, d = x.shape
    tm = TM_OUT
    return pl.pallas_call(
        _out_kernel,
        grid=(b, s // tm),
        in_specs=[
            pl.BlockSpec((1, tm, SEC), lambda bi, i: (bi, i, 0)),
            pl.BlockSpec((1, tm, SEC), lambda bi, i: (bi, i, 0)),
            pl.BlockSpec((1, tm, d), lambda bi, i: (bi, i, 0)),
            pl.BlockSpec((1, 1, d), lambda bi, i: (bi, 0, 0)),
            pl.BlockSpec((2 * SEC, d), lambda bi, i: (0, 0), pipeline_mode=pl.Buffered(1)),
            pl.BlockSpec((1, d), lambda bi, i: (0, 0)),
        ],
        out_specs=pl.BlockSpec((1, tm, d), lambda bi, i: (bi, i, 0)),
        out_shape=jax.ShapeDtypeStruct((b, s, d), F32),
        scratch_shapes=[pltpu.VMEM((2 * SEC, d), BF16)],
        compiler_params=pltpu.CompilerParams(
            dimension_semantics=("arbitrary", "arbitrary"), vmem_limit_bytes=VMEM_LIMIT),
        name="out_proj_residual_norm",
    )(ya, yb, x, gate, w_out, final_g)


def kernel(x, c, positions, w_ada, b_ada, norm_g, w_in, gm_ln_g, gm_ln_b, gm_ws, gm_bs,
           lam_q1, lam_k1, lam_q2, lam_k2, diff_subln_g, w_out, final_g):
    b, s, d = x.shape
    depth = w_ada.shape[0]
    half = DIFF_HEAD_DIM // 2
    inv = ROPE_THETA ** (-jnp.arange(half, dtype=F32) * 2.0 / DIFF_HEAD_DIM)
    inv128 = jnp.tile(inv, LANES // half)[None, :]
    pos4 = positions.reshape(b, s // TM_PROJ, TM_PROJ // LANES, LANES)
    c8 = jnp.zeros((8, d), F32).at[:b].set(c)
    assert depth == 1
    l = 0
    mod = _modulation(c8, w_ada[l], b_ada[l][None, :])[:b]
    shift = mod[:, None, :d]
    scale = mod[:, None, d:2 * d]
    gate = mod[:, None, 2 * d:]
    ya, q, k, vt, zb = _projection(
        x, pos4, scale, shift, norm_g[l][None, :], w_in[l], inv128,
        gm_ln_g[l][None, :], gm_ln_b[l][None, :], gm_ws[l], gm_bs[l][:, :, None])
    lam_init = 0.8 - 0.6 * math.exp(-0.3 * l)
    yb = _attention(q, k, vt, zb, diff_subln_g[l][:, None], lam_q1[l][None, :],
                    lam_k1[l][None, :], lam_q2[l][None, :], lam_k2[l][None, :], lam_init)
    return _out_projection(ya, yb, x, gate, w_out[l], final_g[None, :])
```

```python
import functools
import math

import jax
import jax.numpy as jnp
from jax import lax
from jax.experimental import pallas as pl
from jax.experimental.pallas import tpu as pltpu

F32 = jnp.float32
BF16 = jnp.bfloat16

EPS = 1e-6
CHUNK = 64
GM_GROUPS = 4
GM_GROUP_DIM = 128
GM_WINDOW = 128
DIFF_HEADS = 4
DIFF_HEAD_DIM = 64
DIFF_V_DIM = 128
ROPE_THETA = 10000.0
SEC = 512
N_SEC = 7
LOG2E = 1.4426950408889634

LANES = 128
VMEM_LIMIT = 48 * 1024 * 1024

TM_PROJ = 512
TM_OUT = 1024
OUT_ROW_SPLIT = 256
WEIGHT_CAST_ROWS = 128
TQ_ATT = 256
TK_ATT = 256
G_ATT = 8
SUPER_ATT = TQ_ATT * G_ATT
FULL_TILES_PER_ITER = 8
ONES_ROWS = 16
assert SUPER_ATT % TK_ATT == 0 and TQ_ATT % CHUNK == 0 and TK_ATT % CHUNK == 0
assert (SUPER_ATT // TK_ATT) % FULL_TILES_PER_ITER == 0
assert TM_PROJ % TK_ATT == 0 and TM_PROJ % TQ_ATT == 0


def _silu(x):
    h = 0.5 * x
    return h + h * jnp.tanh(h)


def _gelu(x):
    return 0.5 * x * (1.0 + lax.erf(x * (1.0 / math.sqrt(2.0))))


def _mod_kernel(c_ref, w_ref, b_ref, o_ref):
    cs = _silu(c_ref[...]).astype(BF16)
    w = w_ref[...].astype(BF16)
    o_ref[...] = jnp.dot(cs, w, preferred_element_type=F32) + b_ref[...]


def _modulation(c8, w_ada, b_ada):
    d, n = w_ada.shape
    tn = 1024
    return pl.pallas_call(
        _mod_kernel,
        grid=(n // tn,),
        in_specs=[
            pl.BlockSpec((8, d), lambda j: (0, 0)),
            pl.BlockSpec((d, tn), lambda j: (0, j)),
            pl.BlockSpec((1, tn), lambda j: (0, j)),
        ],
        out_specs=pl.BlockSpec((8, tn), lambda j: (0, j)),
        out_shape=jax.ShapeDtypeStruct((8, n), F32),
        compiler_params=pltpu.CompilerParams(
            dimension_semantics=("arbitrary",), vmem_limit_bytes=VMEM_LIMIT),
        name="adaln_mod",
    )(c8, w_ada, b_ada)


def _cast_weight_once(w_ref, wb_ref):
    first = (pl.program_id(0) == 0) & (pl.program_id(1) == 0)

    @pl.when(first)
    def _():
        def body(i, c):
            rows = pl.ds(pl.multiple_of(i * WEIGHT_CAST_ROWS, WEIGHT_CAST_ROWS), WEIGHT_CAST_ROWS)
            wb_ref[rows, :] = w_ref[rows, :].astype(BF16)
            return c
        lax.fori_loop(0, w_ref.shape[0] // WEIGHT_CAST_ROWS, body, 0)


def _rope_heads(t, cos, sin_signed, lane_lo):
    outs = []
    for hh in range(SEC // LANES):
        xh = t[:, hh * LANES:(hh + 1) * LANES]
        partner = jnp.where(lane_lo, pltpu.roll(xh, LANES - 32, 1), pltpu.roll(xh, 32, 1))
        outs.append(xh * cos + partner * sin_signed)
    return outs


def _proj_kernel(x_ref, pos_ref, scale_ref, shift_ref, ng_ref, w_ref, inv_ref,
                 lng_ref, lnb_ref, ws_ref, bs_ref,
                 ya_ref, qt_ref, k_ref, vt_ref, zb_ref, vln_ref, sv_ref, wb_ref, wsm_ref):
    tm = x_ref.shape[1]
    _cast_weight_once(w_ref, wb_ref)

    @pl.when((pl.program_id(0) == 0) & (pl.program_id(1) == 0))
    def _():
        row_c = lax.broadcasted_iota(jnp.int32, (GM_WINDOW, GM_WINDOW), 0) // CHUNK
        col_c = lax.broadcasted_iota(jnp.int32, (GM_WINDOW, GM_WINDOW), 1) // CHUNK
        for g in range(GM_GROUPS):
            wsm_ref[g] = jnp.where(row_c >= col_c, ws_ref[g], 0.0).astype(BF16)

    pos = pos_ref[0, 0].astype(F32)
    angs = []
    for a in range(tm // LANES):
        col = jnp.broadcast_to(pos[a:a + 1, :], (LANES, LANES)).T
        angs.append(col * inv_ref[...])
    ang = jnp.concatenate(angs, axis=0)
    cos = jnp.cos(ang)
    sin = jnp.sin(ang)
    lane = lax.broadcasted_iota(jnp.int32, (1, LANES), 1)
    lane_lo = (lane % DIFF_HEAD_DIM) < (DIFF_HEAD_DIM // 2)
    sin_signed = jnp.where(lane_lo, -sin, sin)

    x = x_ref[0]
    ms = jnp.mean(x * x, axis=-1, keepdims=True)
    gain = ng_ref[...] * (1.0 + scale_ref[0])
    h = (x * lax.rsqrt(ms + EPS)) * gain + shift_ref[0]
    hb = h.astype(BF16)

    def proj(sec):
        return jnp.dot(hb, wb_ref[:, sec * SEC:(sec + 1) * SEC], preferred_element_type=F32)

    pv = proj(1)
    pu = proj(0)
    v = _gelu(pv)
    n_win = tm // GM_WINDOW
    for g in range(GM_GROUPS):
        cols = slice(g * LANES, (g + 1) * LANES)
        vg = v[:, cols]
        mu = jnp.mean(vg, axis=-1, keepdims=True)
        vc = vg - mu
        var = jnp.mean(vc * vc, axis=-1, keepdims=True)
        vn = vc * lax.rsqrt(var + EPS)
        vn = vn * lng_ref[:, cols] + lnb_ref[:, cols]
        vln_ref[:, cols] = vn.astype(BF16)
        vwin = jnp.concatenate(
            [vln_ref[w * GM_WINDOW:(w + 1) * GM_WINDOW, cols] for w in range(n_win)], axis=1)
        svg = jnp.dot(wsm_ref[g], vwin, preferred_element_type=F32) + bs_ref[g]
        for w in range(n_win):
            sv_ref[w * GM_WINDOW:(w + 1) * GM_WINDOW, cols] = svg[:, w * LANES:(w + 1) * LANES]
    pz = proj(2)
    gu = _gelu(pu)

    pq = proj(3)
    ya_ref[0] = (gu * sv_ref[...] * _silu(pz)).astype(BF16)
    pk = proj(4)
    qscale = (DIFF_HEAD_DIM ** -0.5) * LOG2E
    for hh, qh in enumerate(_rope_heads(pq, cos, sin_signed, lane_lo)):
        for sub in range(tm // TQ_ATT):
            blk = qh[sub * TQ_ATT:(sub + 1) * TQ_ATT, :] * qscale
            qt_ref[0, hh, sub] = blk.T.astype(BF16)
    pvd = proj(5)
    for hh, kh in enumerate(_rope_heads(pk, cos, sin_signed, lane_lo)):
        k_ref[0, hh] = kh.astype(BF16)
    pzb = proj(6)
    for hh in range(DIFF_HEADS):
        for sub in range(tm // TK_ATT):
            blk = pvd[sub * TK_ATT:(sub + 1) * TK_ATT, hh * LANES:(hh + 1) * LANES]
            vt_ref[0, hh, sub] = blk.T.astype(BF16)
    zb_ref[0] = _silu(pzb).astype(BF16)


def _projection(x, pos4, scale, shift, norm_g, w_in, inv128, ln_g, ln_b, ws, bs3):
    b, s, d = x.shape
    tm = TM_PROJ
    head_shape = jax.ShapeDtypeStruct((b, DIFF_HEADS, s, LANES), BF16)
    row_shape = jax.ShapeDtypeStruct((b, s, SEC), BF16)
    head_spec = pl.BlockSpec((1, DIFF_HEADS, tm, LANES), lambda bi, i: (bi, 0, i, 0))
    row_spec = pl.BlockSpec((1, tm, SEC), lambda bi, i: (bi, i, 0))
    vt_shape = jax.ShapeDtypeStruct((b, DIFF_HEADS, s // TK_ATT, DIFF_V_DIM, TK_ATT), BF16)
    vt_spec = pl.BlockSpec((1, DIFF_HEADS, tm // TK_ATT, DIFF_V_DIM, TK_ATT),
                           lambda bi, i: (bi, 0, i, 0, 0))
    qt_shape = jax.ShapeDtypeStruct((b, DIFF_HEADS, s // TQ_ATT, LANES, TQ_ATT), BF16)
    qt_spec = pl.BlockSpec((1, DIFF_HEADS, tm // TQ_ATT, LANES, TQ_ATT),
                           lambda bi, i: (bi, 0, i, 0, 0))
    const2 = lambda bi, i: (0, 0)
    const3 = lambda bi, i: (0, 0, 0)
    return pl.pallas_call(
        _proj_kernel,
        grid=(b, s // tm),
        in_specs=[
            pl.BlockSpec((1, tm, d), lambda bi, i: (bi, i, 0)),
            pl.BlockSpec((1, 1, tm // LANES, LANES), lambda bi, i: (bi, i, 0, 0)),
            pl.BlockSpec((1, 1, d), lambda bi, i: (bi, 0, 0)),
            pl.BlockSpec((1, 1, d), lambda bi, i: (bi, 0, 0)),
            pl.BlockSpec((1, d), const2),
            pl.BlockSpec((d, N_SEC * SEC), const2,
                         pipeline_mode=pl.Buffered(1)),
            pl.BlockSpec((1, LANES), const2),
            pl.BlockSpec((1, SEC), const2),
            pl.BlockSpec((1, SEC), const2),
            pl.BlockSpec((GM_GROUPS, GM_WINDOW, GM_WINDOW), const3),
            pl.BlockSpec((GM_GROUPS, GM_WINDOW, 1), const3),
        ],
        out_specs=[row_spec, qt_spec, head_spec, vt_spec, row_spec],
        out_shape=[row_shape, qt_shape, head_shape, vt_shape, row_shape],
        scratch_shapes=[pltpu.VMEM((tm, SEC), BF16), pltpu.VMEM((tm, SEC), F32),
                        pltpu.VMEM((d, N_SEC * SEC), BF16),
                        pltpu.VMEM((GM_GROUPS, GM_WINDOW, GM_WINDOW), BF16)],
        compiler_params=pltpu.CompilerParams(
            dimension_semantics=("arbitrary", "arbitrary"), vmem_limit_bytes=VMEM_LIMIT),
        name="in_proj_gmlp_rope",
    )(x, pos4, scale, shift, norm_g, w_in, inv128, ln_g, ln_b, ws, bs3)


def _attn_kernel(lam_init, qt_ref, k_ref, vt_ref, zb_ref, g_ref, lq1_ref, lk1_ref, lq2_ref,
                 lk2_ref, o_ref, q2_ref, acc_ref, m_ref, s_carry, p_carry, a_carry):
    s_len = k_ref.shape[2]
    tq, tk, ng = TQ_ATT, TK_ATT, G_ATT
    n_super = s_len // SUPER_ATT
    tiles_per_super = SUPER_ATT // tk

    lam = (jnp.exp(jnp.sum(lq1_ref[...] * lk1_ref[...], axis=-1, keepdims=True))
           - jnp.exp(jnp.sum(lq2_ref[...] * lk2_ref[...], axis=-1, keepdims=True))
           + lam_init)

    norm_gain = g_ref[...] * (1.0 - lam_init)
    first_feat = lax.broadcasted_iota(jnp.int32, (LANES, 1), 0) < DIFF_HEAD_DIM
    key_pos = lax.broadcasted_iota(jnp.int32, (tk, 2 * tq), 0)
    qry_pos = lax.broadcasted_iota(jnp.int32, (tk, 2 * tq), 1) % tq
    ones_rows = jnp.ones((ONES_ROWS, tk), BF16)

    def allowed(q_off):
        return key_pos // CHUNK <= (qry_pos + q_off) // CHUNK

    def scores(job):
        g, j, _ = job
        kt = k_ref[0, 0, pl.ds(pl.multiple_of(j * tk, tk), tk), :]
        return jnp.dot(kt, q2_ref[g], preferred_element_type=F32)

    def softmax_update(job, s):
        g, _, q_off = job
        if q_off is not None:
            s = jnp.where(allowed(q_off), s, -jnp.inf)
        m_prev = m_ref[g]
        m_new = jnp.maximum(m_prev, jnp.max(s, axis=0, keepdims=True))
        m_ref[g] = m_new
        return jnp.exp2(m_prev - m_new), jnp.exp2(s - m_new).astype(BF16)

    def accumulate(job, alpha, p):
        g, j, _ = job
        vte = jnp.concatenate([vt_ref[0, 0, j], ones_rows], axis=0)
        acc_ref[g] = alpha * acc_ref[g] + jnp.dot(vte, p, preferred_element_type=F32)

    def run_stream(jobs, lookahead, pending_job):
        s_cur = s_carry[...]
        pend = (pending_job, a_carry[...], p_carry[...])
        for i, job in enumerate(jobs):
            nxt = jobs[i + 1] if i + 1 < len(jobs) else lookahead
            s_nxt = scores(nxt) if nxt is not None else None
            alpha, p = softmax_update(job, s_cur)
            accumulate(*pend)
            pend = (job, alpha, p)
            s_cur = s_nxt
        if lookahead is None:
            accumulate(*pend)
        else:
            s_carry[...] = s_cur
            a_carry[...] = pend[1]
            p_carry[...] = pend[2]

    def load_queries(g, si):
        qt = qt_ref[0, 0, si * ng + g]
        zero = jnp.zeros_like(qt)
        q2_ref[g, :, :tq] = jnp.where(first_feat, qt, zero)
        q2_ref[g, :, tq:] = jnp.where(first_feat, zero, qt)

    def reset_chain(g):
        m_ref[g] = jnp.full(m_ref.shape[1:], -jnp.inf, F32)
        acc_ref[g] = jnp.zeros(acc_ref.shape[1:], F32)

    def finish_chain(g, si):
        acc = acc_ref[g]
        o_all = acc[:DIFF_V_DIM] * (1.0 / acc[DIFF_V_DIM:DIFF_V_DIM + 1])
        o = o_all[:, :tq] - lam * o_all[:, tq:]
        ms = jnp.mean(o * o, axis=0, keepdims=True)
        on = ((o * lax.rsqrt(ms + EPS)) * norm_gain).T
        rows = pl.ds(pl.multiple_of(si * SUPER_ATT, SUPER_ATT) + g * tq, tq)
        o_ref[0, rows, :] = (on * zb_ref[0, rows, :].astype(F32)).astype(BF16)

    def full_tiles(jj, c):
        j0 = jj * FULL_TILES_PER_ITER
        jobs = [(g, j0 + u, None) for u in range(FULL_TILES_PER_ITER) for g in range(ng)]
        run_stream(jobs, (0, j0 + FULL_TILES_PER_ITER, None),
                   (ng - 1, jnp.maximum(j0 - 1, 0), None))
        return c

    def super_block(si, carry):
        for g in range(ng):
            load_queries(g, si)
            reset_chain(g)
        s_carry[...] = scores((0, 0, None))
        a_carry[...] = jnp.ones(a_carry.shape, F32)
        p_carry[...] = jnp.zeros(p_carry.shape, BF16)
        n_full = si * (tiles_per_super // FULL_TILES_PER_ITER)
        lax.fori_loop(0, n_full, full_tiles, 0)

        j0 = si * tiles_per_super
        jobs = []
        for bt in range(tiles_per_super):
            for g in range(ng):
                q_off = g * tq - bt * tk
                if q_off + tq <= 0:
                    continue
                needs_mask = q_off // CHUNK < (tk - 1) // CHUNK
                jobs.append((g, j0 + bt, q_off if needs_mask else None))
        assert jobs[0][0] == 0 and jobs[-1][0] == ng - 1
        run_stream(jobs, None, (ng - 1, jnp.maximum(j0 - 1, 0), None))
        for g in range(ng):
            finish_chain(g, si)
        return carry

    lax.fori_loop(0, n_super, super_block, 0)


def _attention(qt, k, vt, zb, subln_g_col, lq1, lk1, lq2, lk2, lam_init):
    b, h, s, _ = k.shape
    head_spec = pl.BlockSpec((1, 1, s, LANES), lambda bi, hi: (bi, hi, 0, 0))
    qt_spec = pl.BlockSpec((1, 1, s // TQ_ATT, LANES, TQ_ATT), lambda bi, hi: (bi, hi, 0, 0, 0))
    vt_spec = pl.BlockSpec((1, 1, s // TK_ATT, DIFF_V_DIM, TK_ATT),
                           lambda bi, hi: (bi, hi, 0, 0, 0))
    col_spec = pl.BlockSpec((1, s, LANES), lambda bi, hi: (bi, 0, hi))
    vec64 = pl.BlockSpec((1, DIFF_HEAD_DIM), lambda bi, hi: (0, 0))
    return pl.pallas_call(
        functools.partial(_attn_kernel, lam_init),
        grid=(b, h),
        in_specs=[qt_spec, head_spec, vt_spec, col_spec,
                  pl.BlockSpec((DIFF_V_DIM, 1), lambda bi, hi: (0, 0)),
                  vec64, vec64, vec64, vec64],
        out_specs=col_spec,
        out_shape=jax.ShapeDtypeStruct((b, s, h * LANES), BF16),
        scratch_shapes=[pltpu.VMEM((G_ATT, LANES, 2 * TQ_ATT), BF16),
                        pltpu.VMEM((G_ATT, DIFF_V_DIM + ONES_ROWS, 2 * TQ_ATT), F32),
                        pltpu.VMEM((G_ATT, 1, 2 * TQ_ATT), F32),
                        pltpu.VMEM((TK_ATT, 2 * TQ_ATT), F32),
                        pltpu.VMEM((TK_ATT, 2 * TQ_ATT), BF16),
                        pltpu.VMEM((1, 2 * TQ_ATT), F32)],
        compiler_params=pltpu.CompilerParams(
            dimension_semantics=("arbitrary", "arbitrary"), vmem_limit_bytes=VMEM_LIMIT),
        name="diff_attention",
    )(qt, k, vt, zb, subln_g_col, lq1, lk1, lq2, lk2)


def _out_kernel(ya_ref, yb_ref, x_ref, gate_ref, w_ref, fg_ref, o_ref, wb_ref):
    _cast_weight_once(w_ref, wb_ref)
    tm = x_ref.shape[1]
    n_split = tm // OUT_ROW_SPLIT
    ys = []
    for r in range(n_split):
        rows = slice(r * OUT_ROW_SPLIT, (r + 1) * OUT_ROW_SPLIT)
        y = jnp.dot(ya_ref[0, rows, :], wb_ref[:SEC, :], preferred_element_type=F32)
        ys.append(y + jnp.dot(yb_ref[0, rows, :], wb_ref[SEC:, :], preferred_element_type=F32))
    for r in range(n_split):
        rows = slice(r * OUT_ROW_SPLIT, (r + 1) * OUT_ROW_SPLIT)
        xo = x_ref[0, rows, :] + gate_ref[0] * ys[r]
        ms = jnp.mean(xo * xo, axis=-1, keepdims=True)
        o_ref[0, rows, :] = (xo * lax.rsqrt(ms + EPS)) * fg_ref[...]


def _out_projection(ya, yb, x, gate, w_out, final_g):
    b, s, d = x.shape
    tm = TM_OUT
    return pl.pallas_call(
        _out_kernel,
        grid=(b, s // tm),
        in_specs=[
            pl.BlockSpec((1, tm, SEC), lambda bi, i: (bi, i, 0)),
            pl.BlockSpec((1, tm, SEC), lambda bi, i: (bi, i, 0)),
            pl.BlockSpec((1, tm, d), lambda bi, i: (bi, i, 0)),
            pl.BlockSpec((1, 1, d), lambda bi, i: (bi, 0, 0)),
            pl.BlockSpec((2 * SEC, d), lambda bi, i: (0, 0), pipeline_mode=pl.Buffered(1)),
            pl.BlockSpec((1, d), lambda bi, i: (0, 0)),
        ],
        out_specs=pl.BlockSpec((1, tm, d), lambda bi, i: (bi, i, 0)),
        out_shape=jax.ShapeDtypeStruct((b, s, d), F32),
        scratch_shapes=[pltpu.VMEM((2 * SEC, d), BF16)],
        compiler_params=pltpu.CompilerParams(
            dimension_semantics=("arbitrary", "arbitrary"), vmem_limit_bytes=VMEM_LIMIT),
        name="out_proj_residual_norm",
    )(ya, yb, x, gate, w_out, final_g)


def kernel(x, c, positions, w_ada, b_ada, norm_g, w_in, gm_ln_g, gm_ln_b, gm_ws, gm_bs,
           lam_q1, lam_k1, lam_q2, lam_k2, diff_subln_g, w_out, final_g):
    b, s, d = x.shape
    depth = w_ada.shape[0]
    half = DIFF_HEAD_DIM // 2
    inv = ROPE_THETA ** (-jnp.arange(half, dtype=F32) * 2.0 / DIFF_HEAD_DIM)
    inv128 = jnp.tile(inv, LANES // half)[None, :]
    pos4 = positions.reshape(b, s // TM_PROJ, TM_PROJ // LANES, LANES)
    c8 = jnp.zeros((8, d), F32).at[:b].set(c)
    assert depth == 1
    l = 0
    mod = _modulation(c8, w_ada[l], b_ada[l][None, :])[:b]
    shift = mod[:, None, :d]
    scale = mod[:, None, d:2 * d]
    gate = mod[:, None, 2 * d:]
    ya, q, k, vt, zb = _projection(
        x, pos4, scale, shift, norm_g[l][None, :], w_in[l], inv128,
        gm_ln_g[l][None, :], gm_ln_b[l][None, :], gm_ws[l], gm_bs[l][:, :, None])
    lam_init = 0.8 - 0.6 * math.exp(-0.3 * l)
    yb = _attention(q, k, vt, zb, diff_subln_g[l][:, None], lam_q1[l][None, :],
                    lam_k1[l][None, :], lam_q2[l][None, :], lam_k2[l][None, :], lam_init)
    return _out_projection(ya, yb, x, gate, w_out[l], final_g[None, :])
```

```python
import functools
import math

import jax
import jax.numpy as jnp
from jax import lax
from jax.experimental import pallas as pl
from jax.experimental.pallas import tpu as pltpu

F32 = jnp.float32
BF16 = jnp.bfloat16

EPS = 1e-6
CHUNK = 64
GM_GROUPS = 4
GM_GROUP_DIM = 128
GM_WINDOW = 128
DIFF_HEADS = 4
DIFF_HEAD_DIM = 64
DIFF_V_DIM = 128
ROPE_THETA = 10000.0
SEC = 512
N_SEC = 7
LOG2E = 1.4426950408889634

LANES = 128
VMEM_LIMIT = 48 * 1024 * 1024

TM_PROJ = 512
TM_OUT = 1024
OUT_ROW_SPLIT = 256
X_RING_SLOTS = 3
WEIGHT_CAST_ROWS = 128
TQ_ATT = 256
TK_ATT = 256
G_ATT = 8
SUPER_ATT = TQ_ATT * G_ATT
FULL_TILES_PER_ITER = 8
ONES_ROWS = 16
assert SUPER_ATT % TK_ATT == 0 and TQ_ATT % CHUNK == 0 and TK_ATT % CHUNK == 0
assert (SUPER_ATT // TK_ATT) % FULL_TILES_PER_ITER == 0
assert TM_PROJ % TK_ATT == 0 and TM_PROJ % TQ_ATT == 0


def _silu(x):
    h = 0.5 * x
    return h + h * jnp.tanh(h)


def _gelu(x):
    return 0.5 * x * (1.0 + lax.erf(x * (1.0 / math.sqrt(2.0))))


def _mod_kernel(c_ref, w_ref, b_ref, o_ref):
    cs = _silu(c_ref[...]).astype(BF16)
    w = w_ref[...].astype(BF16)
    o_ref[...] = jnp.dot(cs, w, preferred_element_type=F32) + b_ref[...]


def _modulation(c8, w_ada, b_ada):
    d, n = w_ada.shape
    tn = 1024
    return pl.pallas_call(
        _mod_kernel,
        grid=(n // tn,),
        in_specs=[
            pl.BlockSpec((8, d), lambda j: (0, 0)),
            pl.BlockSpec((d, tn), lambda j: (0, j)),
            pl.BlockSpec((1, tn), lambda j: (0, j)),
        ],
        out_specs=pl.BlockSpec((8, tn), lambda j: (0, j)),
        out_shape=jax.ShapeDtypeStruct((8, n), F32),
        compiler_params=pltpu.CompilerParams(
            dimension_semantics=("arbitrary",), vmem_limit_bytes=VMEM_LIMIT),
        name="adaln_mod",
    )(c8, w_ada, b_ada)


def _cast_weight_once(w_ref, wb_ref):
    first = (pl.program_id(0) == 0) & (pl.program_id(1) == 0)

    @pl.when(first)
    def _():
        def body(i, c):
            rows = pl.ds(pl.multiple_of(i * WEIGHT_CAST_ROWS, WEIGHT_CAST_ROWS), WEIGHT_CAST_ROWS)
            wb_ref[rows, :] = w_ref[rows, :].astype(BF16)
            return c
        lax.fori_loop(0, w_ref.shape[0] // WEIGHT_CAST_ROWS, body, 0)


def _rope_heads(t, cos, sin_signed, lane_lo):
    outs = []
    for hh in range(SEC // LANES):
        xh = t[:, hh * LANES:(hh + 1) * LANES]
        partner = jnp.where(lane_lo, pltpu.roll(xh, LANES - 32, 1), pltpu.roll(xh, 32, 1))
        outs.append(xh * cos + partner * sin_signed)
    return outs


def _proj_kernel(x_ref, pos_ref, scale_ref, shift_ref, ng_ref, w_ref, inv_ref,
                 lng_ref, lnb_ref, ws_ref, bs_ref,
                 ya_ref, qt_ref, k_ref, vt_ref, zb_ref, vln_ref, sv_ref, wb_ref, wsm_ref):
    tm = x_ref.shape[1]
    _cast_weight_once(w_ref, wb_ref)

    @pl.when((pl.program_id(0) == 0) & (pl.program_id(1) == 0))
    def _():
        row_c = lax.broadcasted_iota(jnp.int32, (GM_WINDOW, GM_WINDOW), 0) // CHUNK
        col_c = lax.broadcasted_iota(jnp.int32, (GM_WINDOW, GM_WINDOW), 1) // CHUNK
        for g in range(GM_GROUPS):
            wsm_ref[g] = jnp.where(row_c >= col_c, ws_ref[g], 0.0).astype(BF16)

    pos = pos_ref[0, 0].astype(F32)
    angs = []
    for a in range(tm // LANES):
        col = jnp.broadcast_to(pos[a:a + 1, :], (LANES, LANES)).T
        angs.append(col * inv_ref[...])
    ang = jnp.concatenate(angs, axis=0)
    cos = jnp.cos(ang)
    sin = jnp.sin(ang)
    lane = lax.broadcasted_iota(jnp.int32, (1, LANES), 1)
    lane_lo = (lane % DIFF_HEAD_DIM) < (DIFF_HEAD_DIM // 2)
    sin_signed = jnp.where(lane_lo, -sin, sin)

    x = x_ref[0]
    ms = jnp.mean(x * x, axis=-1, keepdims=True)
    gain = ng_ref[...] * (1.0 + scale_ref[0])
    h = (x * lax.rsqrt(ms + EPS)) * gain + shift_ref[0]
    hb = h.astype(BF16)

    def proj(sec):
        return jnp.dot(hb, wb_ref[:, sec * SEC:(sec + 1) * SEC], preferred_element_type=F32)

    pv = proj(1)
    pu = proj(0)
    v = _gelu(pv)
    n_win = tm // GM_WINDOW
    for g in range(GM_GROUPS):
        cols = slice(g * LANES, (g + 1) * LANES)
        vg = v[:, cols]
        mu = jnp.mean(vg, axis=-1, keepdims=True)
        vc = vg - mu
        var = jnp.mean(vc * vc, axis=-1, keepdims=True)
        vn = vc * lax.rsqrt(var + EPS)
        vn = vn * lng_ref[:, cols] + lnb_ref[:, cols]
        vln_ref[:, cols] = vn.astype(BF16)
        vwin = jnp.concatenate(
            [vln_ref[w * GM_WINDOW:(w + 1) * GM_WINDOW, cols] for w in range(n_win)], axis=1)
        svg = jnp.dot(wsm_ref[g], vwin, preferred_element_type=F32) + bs_ref[g]
        for w in range(n_win):
            sv_ref[w * GM_WINDOW:(w + 1) * GM_WINDOW, cols] = svg[:, w * LANES:(w + 1) * LANES]
    pz = proj(2)
    gu = _gelu(pu)

    pq = proj(3)
    ya_ref[0] = (gu * sv_ref[...] * _silu(pz)).astype(BF16)
    pk = proj(4)
    qscale = (DIFF_HEAD_DIM ** -0.5) * LOG2E
    for hh, qh in enumerate(_rope_heads(pq, cos, sin_signed, lane_lo)):
        for sub in range(tm // TQ_ATT):
            blk = qh[sub * TQ_ATT:(sub + 1) * TQ_ATT, :] * qscale
            qt_ref[0, hh, sub] = blk.T.astype(BF16)
    pvd = proj(5)
    for hh, kh in enumerate(_rope_heads(pk, cos, sin_signed, lane_lo)):
        k_ref[0, hh] = kh.astype(BF16)
    pzb = proj(6)
    for hh in range(DIFF_HEADS):
        for sub in range(tm // TK_ATT):
            blk = pvd[sub * TK_ATT:(sub + 1) * TK_ATT, hh * LANES:(hh + 1) * LANES]
            vt_ref[0, hh, sub] = blk.T.astype(BF16)
    zb_ref[0] = _silu(pzb).astype(BF16)


def _projection(x, pos4, scale, shift, norm_g, w_in, inv128, ln_g, ln_b, ws, bs3):
    b, s, d = x.shape
    tm = TM_PROJ
    head_shape = jax.ShapeDtypeStruct((b, DIFF_HEADS, s, LANES), BF16)
    row_shape = jax.ShapeDtypeStruct((b, s, SEC), BF16)
    head_spec = pl.BlockSpec((1, DIFF_HEADS, tm, LANES), lambda bi, i: (bi, 0, i, 0))
    row_spec = pl.BlockSpec((1, tm, SEC), lambda bi, i: (bi, i, 0))
    vt_shape = jax.ShapeDtypeStruct((b, DIFF_HEADS, s // TK_ATT, DIFF_V_DIM, TK_ATT), BF16)
    vt_spec = pl.BlockSpec((1, DIFF_HEADS, tm // TK_ATT, DIFF_V_DIM, TK_ATT),
                           lambda bi, i: (bi, 0, i, 0, 0))
    qt_shape = jax.ShapeDtypeStruct((b, DIFF_HEADS, s // TQ_ATT, LANES, TQ_ATT), BF16)
    qt_spec = pl.BlockSpec((1, DIFF_HEADS, tm // TQ_ATT, LANES, TQ_ATT),
                           lambda bi, i: (bi, 0, i, 0, 0))
    const2 = lambda bi, i: (0, 0)
    const3 = lambda bi, i: (0, 0, 0)
    return pl.pallas_call(
        _proj_kernel,
        grid=(b, s // tm),
        in_specs=[
            pl.BlockSpec((1, tm, d), lambda bi, i: (bi, i, 0)),
            pl.BlockSpec((1, 1, tm // LANES, LANES), lambda bi, i: (bi, i, 0, 0)),
            pl.BlockSpec((1, 1, d), lambda bi, i: (bi, 0, 0)),
            pl.BlockSpec((1, 1, d), lambda bi, i: (bi, 0, 0)),
            pl.BlockSpec((1, d), const2),
            pl.BlockSpec((d, N_SEC * SEC), const2,
                         pipeline_mode=pl.Buffered(1)),
            pl.BlockSpec((1, LANES), const2),
            pl.BlockSpec((1, SEC), const2),
            pl.BlockSpec((1, SEC), const2),
            pl.BlockSpec((GM_GROUPS, GM_WINDOW, GM_WINDOW), const3),
            pl.BlockSpec((GM_GROUPS, GM_WINDOW, 1), const3),
        ],
        out_specs=[row_spec, qt_spec, head_spec, vt_spec, row_spec],
        out_shape=[row_shape, qt_shape, head_shape, vt_shape, row_shape],
        scratch_shapes=[pltpu.VMEM((tm, SEC), BF16), pltpu.VMEM((tm, SEC), F32),
                        pltpu.VMEM((d, N_SEC * SEC), BF16),
                        pltpu.VMEM((GM_GROUPS, GM_WINDOW, GM_WINDOW), BF16)],
        compiler_params=pltpu.CompilerParams(
            dimension_semantics=("arbitrary", "arbitrary"), vmem_limit_bytes=VMEM_LIMIT),
        name="in_proj_gmlp_rope",
    )(x, pos4, scale, shift, norm_g, w_in, inv128, ln_g, ln_b, ws, bs3)


def _attn_kernel(lam_init, qt_ref, k_ref, vt_ref, zb_ref, g_ref, lq1_ref, lk1_ref, lq2_ref,
                 lk2_ref, o_ref, q2_ref, acc_ref, m_ref, s_carry, p_carry, a_carry):
    s_len = k_ref.shape[2]
    tq, tk, ng = TQ_ATT, TK_ATT, G_ATT
    n_super = s_len // SUPER_ATT
    tiles_per_super = SUPER_ATT // tk

    lam = (jnp.exp(jnp.sum(lq1_ref[...] * lk1_ref[...], axis=-1, keepdims=True))
           - jnp.exp(jnp.sum(lq2_ref[...] * lk2_ref[...], axis=-1, keepdims=True))
           + lam_init)

    norm_gain = g_ref[...] * (1.0 - lam_init)
    first_feat = lax.broadcasted_iota(jnp.int32, (LANES, 1), 0) < DIFF_HEAD_DIM
    key_pos = lax.broadcasted_iota(jnp.int32, (tk, 2 * tq), 0)
    qry_pos = lax.broadcasted_iota(jnp.int32, (tk, 2 * tq), 1) % tq
    ones_rows = jnp.ones((ONES_ROWS, tk), BF16)

    def allowed(q_off):
        return key_pos // CHUNK <= (qry_pos + q_off) // CHUNK

    def scores(job):
        g, j, _ = job
        kt = k_ref[0, 0, pl.ds(pl.multiple_of(j * tk, tk), tk), :]
        return jnp.dot(kt, q2_ref[g], preferred_element_type=F32)

    def softmax_update(job, s):
        g, _, q_off = job
        if q_off is not None:
            s = jnp.where(allowed(q_off), s, -jnp.inf)
        m_prev = m_ref[g]
        m_new = jnp.maximum(m_prev, jnp.max(s, axis=0, keepdims=True))
        m_ref[g] = m_new
        return jnp.exp2(m_prev - m_new), jnp.exp2(s - m_new).astype(BF16)

    def accumulate(job, alpha, p):
        g, j, _ = job
        vte = jnp.concatenate([vt_ref[0, 0, j], ones_rows], axis=0)
        acc_ref[g] = alpha * acc_ref[g] + jnp.dot(vte, p, preferred_element_type=F32)

    def run_stream(jobs, lookahead, pending_job):
        s_cur = s_carry[...]
        pend = (pending_job, a_carry[...], p_carry[...])
        for i, job in enumerate(jobs):
            nxt = jobs[i + 1] if i + 1 < len(jobs) else lookahead
            s_nxt = scores(nxt) if nxt is not None else None
            alpha, p = softmax_update(job, s_cur)
            accumulate(*pend)
            pend = (job, alpha, p)
            s_cur = s_nxt
        if lookahead is None:
            accumulate(*pend)
        else:
            s_carry[...] = s_cur
            a_carry[...] = pend[1]
            p_carry[...] = pend[2]

    def load_queries(g, si):
        qt = qt_ref[0, 0, si * ng + g]
        zero = jnp.zeros_like(qt)
        q2_ref[g, :, :tq] = jnp.where(first_feat, qt, zero)
        q2_ref[g, :, tq:] = jnp.where(first_feat, zero, qt)

    def reset_chain(g):
        m_ref[g] = jnp.full(m_ref.shape[1:], -jnp.inf, F32)
        acc_ref[g] = jnp.zeros(acc_ref.shape[1:], F32)

    def finish_chain(g, si):
        acc = acc_ref[g]
        o_all = acc[:DIFF_V_DIM] * (1.0 / acc[DIFF_V_DIM:DIFF_V_DIM + 1])
        o = o_all[:, :tq] - lam * o_all[:, tq:]
        ms = jnp.mean(o * o, axis=0, keepdims=True)
        on = ((o * lax.rsqrt(ms + EPS)) * norm_gain).T
        rows = pl.ds(pl.multiple_of(si * SUPER_ATT, SUPER_ATT) + g * tq, tq)
        o_ref[0, rows, :] = (on * zb_ref[0, rows, :].astype(F32)).astype(BF16)

    def full_tiles(jj, c):
        j0 = jj * FULL_TILES_PER_ITER
        jobs = [(g, j0 + u, None) for u in range(FULL_TILES_PER_ITER) for g in range(ng)]
        run_stream(jobs, (0, j0 + FULL_TILES_PER_ITER, None),
                   (ng - 1, jnp.maximum(j0 - 1, 0), None))
        return c

    def super_block(si, carry):
        for g in range(ng):
            load_queries(g, si)
            reset_chain(g)
        s_carry[...] = scores((0, 0, None))
        a_carry[...] = jnp.ones(a_carry.shape, F32)
        p_carry[...] = jnp.zeros(p_carry.shape, BF16)
        n_full = si * (tiles_per_super // FULL_TILES_PER_ITER)
        lax.fori_loop(0, n_full, full_tiles, 0)

        j0 = si * tiles_per_super
        jobs = []
        for bt in range(tiles_per_super):
            for g in range(ng):
                q_off = g * tq - bt * tk
                if q_off + tq <= 0:
                    continue
                needs_mask = q_off // CHUNK < (tk - 1) // CHUNK
                jobs.append((g, j0 + bt, q_off if needs_mask else None))
        assert jobs[0][0] == 0 and jobs[-1][0] == ng - 1
        run_stream(jobs, None, (ng - 1, jnp.maximum(j0 - 1, 0), None))
        for g in range(ng):
            finish_chain(g, si)
        return carry

    lax.fori_loop(0, n_super, super_block, 0)


def _attention(qt, k, vt, zb, subln_g_col, lq1, lk1, lq2, lk2, lam_init):
    b, h, s, _ = k.shape
    head_spec = pl.BlockSpec((1, 1, s, LANES), lambda bi, hi: (bi, hi, 0, 0))
    qt_spec = pl.BlockSpec((1, 1, s // TQ_ATT, LANES, TQ_ATT), lambda bi, hi: (bi, hi, 0, 0, 0))
    vt_spec = pl.BlockSpec((1, 1, s // TK_ATT, DIFF_V_DIM, TK_ATT),
                           lambda bi, hi: (bi, hi, 0, 0, 0))
    col_spec = pl.BlockSpec((1, s, LANES), lambda bi, hi: (bi, 0, hi))
    vec64 = pl.BlockSpec((1, DIFF_HEAD_DIM), lambda bi, hi: (0, 0))
    return pl.pallas_call(
        functools.partial(_attn_kernel, lam_init),
        grid=(b, h),
        in_specs=[qt_spec, head_spec, vt_spec, col_spec,
                  pl.BlockSpec((DIFF_V_DIM, 1), lambda bi, hi: (0, 0)),
                  vec64, vec64, vec64, vec64],
        out_specs=col_spec,
        out_shape=jax.ShapeDtypeStruct((b, s, h * LANES), BF16),
        scratch_shapes=[pltpu.VMEM((G_ATT, LANES, 2 * TQ_ATT), BF16),
                        pltpu.VMEM((G_ATT, DIFF_V_DIM + ONES_ROWS, 2 * TQ_ATT), F32),
                        pltpu.VMEM((G_ATT, 1, 2 * TQ_ATT), F32),
                        pltpu.VMEM((TK_ATT, 2 * TQ_ATT), F32),
                        pltpu.VMEM((TK_ATT, 2 * TQ_ATT), BF16),
                        pltpu.VMEM((1, 2 * TQ_ATT), F32)],
        compiler_params=pltpu.CompilerParams(
            dimension_semantics=("arbitrary", "arbitrary"), vmem_limit_bytes=VMEM_LIMIT),
        name="diff_attention",
    )(qt, k, vt, zb, subln_g_col, lq1, lk1, lq2, lk2)


def _out_kernel(ya_ref, yb_ref, x_hbm, gate_ref, w_ref, fg_ref, o_ref, wb_ref, xbuf, xsem):
    _cast_weight_once(w_ref, wb_ref)
    tm = o_ref.shape[1]
    n_i = pl.num_programs(1)
    n_steps = pl.num_programs(0) * n_i
    step = pl.program_id(0) * n_i + pl.program_id(1)

    def x_copy(t):
        slot = t % X_RING_SLOTS
        rows = pl.ds(pl.multiple_of((t % n_i) * tm, tm), tm)
        return pltpu.make_async_copy(x_hbm.at[t // n_i, rows, :], xbuf.at[slot], xsem.at[slot])

    @pl.when(step == 0)
    def _():
        for t0 in range(X_RING_SLOTS - 1):
            x_copy(t0).start()

    @pl.when(step + X_RING_SLOTS - 1 < n_steps)
    def _():
        x_copy(step + X_RING_SLOTS - 1).start()

    x_copy(step).wait()
    x_ref = xbuf.at[step % X_RING_SLOTS]
    n_split = tm // OUT_ROW_SPLIT
    ys = []
    for r in range(n_split):
        rows = slice(r * OUT_ROW_SPLIT, (r + 1) * OUT_ROW_SPLIT)
        y = jnp.dot(ya_ref[0, rows, :], wb_ref[:SEC, :], preferred_element_type=F32)
        ys.append(y + jnp.dot(yb_ref[0, rows, :], wb_ref[SEC:, :], preferred_element_type=F32))
    for r in range(n_split):
        rows = slice(r * OUT_ROW_SPLIT, (r + 1) * OUT_ROW_SPLIT)
        xo = x_ref[rows, :] + gate_ref[0] * ys[r]
        ms = jnp.mean(xo * xo, axis=-1, keepdims=True)
        o_ref[0, rows, :] = (xo * lax.rsqrt(ms + EPS)) * fg_ref[...]


def _out_projection(ya, yb, x, gate, w_out, final_g):
    b, s, d = x.shape
    tm = TM_OUT
    return pl.pallas_call(
        _out_kernel,
        grid=(b, s // tm),
        in_specs=[
            pl.BlockSpec((1, tm, SEC), lambda bi, i: (bi, i, 0)),
            pl.BlockSpec((1, tm, SEC), lambda bi, i: (bi, i, 0)),
            pl.BlockSpec(memory_space=pl.ANY),
            pl.BlockSpec((1, 1, d), lambda bi, i: (bi, 0, 0)),
            pl.BlockSpec((2 * SEC, d), lambda bi, i: (0, 0), pipeline_mode=pl.Buffered(1)),
            pl.BlockSpec((1, d), lambda bi, i: (0, 0)),
        ],
        out_specs=pl.BlockSpec((1, tm, d), lambda bi, i: (bi, i, 0)),
        out_shape=jax.ShapeDtypeStruct((b, s, d), F32),
        scratch_shapes=[pltpu.VMEM((2 * SEC, d), BF16),
                        pltpu.VMEM((X_RING_SLOTS, tm, d), F32),
                        pltpu.SemaphoreType.DMA((X_RING_SLOTS,))],
        compiler_params=pltpu.CompilerParams(
            dimension_semantics=("arbitrary", "arbitrary"), vmem_limit_bytes=VMEM_LIMIT),
        name="out_proj_residual_norm",
    )(ya, yb, x, gate, w_out, final_g)


def kernel(x, c, positions, w_ada, b_ada, norm_g, w_in, gm_ln_g, gm_ln_b, gm_ws, gm_bs,
           lam_q1, lam_k1, lam_q2, lam_k2, diff_subln_g, w_out, final_g):
    b, s, d = x.shape
    depth = w_ada.shape[0]
    half = DIFF_HEAD_DIM // 2
    inv = ROPE_THETA ** (-jnp.arange(half, dtype=F32) * 2.0 / DIFF_HEAD_DIM)
    inv128 = jnp.tile(inv, LANES // half)[None, :]
    pos4 = positions.reshape(b, s // TM_PROJ, TM_PROJ // LANES, LANES)
    c8 = jnp.zeros((8, d), F32).at[:b].set(c)
    assert depth == 1
    l = 0
    mod = _modulation(c8, w_ada[l], b_ada[l][None, :])[:b]
    shift = mod[:, None, :d]
    scale = mod[:, None, d:2 * d]
    gate = mod[:, None, 2 * d:]
    ya, q, k, vt, zb = _projection(
        x, pos4, scale, shift, norm_g[l][None, :], w_in[l], inv128,
        gm_ln_g[l][None, :], gm_ln_b[l][None, :], gm_ws[l], gm_bs[l][:, :, None])
    lam_init = 0.8 - 0.6 * math.exp(-0.3 * l)
    yb = _attention(q, k, vt, zb, diff_subln_g[l][:, None], lam_q1[l][None, :],
                    lam_k1[l][None, :], lam_q2[l][None, :], lam_k2[l][None, :], lam_init)
    return _out_projection(ya, yb, x, gate, w_out[l], final_g[None, :])
```

```python
import functools
import math

import jax
import jax.numpy as jnp
from jax import lax
from jax.experimental import pallas as pl
from jax.experimental.pallas import tpu as pltpu

F32 = jnp.float32
BF16 = jnp.bfloat16

EPS = 1e-6
CHUNK = 64
GM_GROUPS = 4
GM_GROUP_DIM = 128
GM_WINDOW = 128
DIFF_HEADS = 4
DIFF_HEAD_DIM = 64
DIFF_V_DIM = 128
ROPE_THETA = 10000.0
SEC = 512
N_SEC = 7
LOG2E = 1.4426950408889634

LANES = 128
VMEM_LIMIT = 48 * 1024 * 1024

TM_PROJ = 512
TM_OUT = 1024
OUT_ROW_SPLIT = 256
X_RING_SLOTS = 3
WEIGHT_CAST_ROWS = 128
TQ_ATT = 256
TK_ATT = 256
G_ATT = 8
SUPER_ATT = TQ_ATT * G_ATT
FULL_TILES_PER_ITER = 8
ONES_ROWS = 16
assert SUPER_ATT % TK_ATT == 0 and TQ_ATT % CHUNK == 0 and TK_ATT % CHUNK == 0
assert (SUPER_ATT // TK_ATT) % FULL_TILES_PER_ITER == 0
assert TM_PROJ % TK_ATT == 0 and TM_PROJ % TQ_ATT == 0


def _silu(x):
    h = 0.5 * x
    return h + h * jnp.tanh(h)


def _gelu(x):
    return 0.5 * x * (1.0 + lax.erf(x * (1.0 / math.sqrt(2.0))))


def _mod_kernel(c_ref, w_ref, b_ref, o_ref):
    cs = _silu(c_ref[...]).astype(BF16)
    w = w_ref[...].astype(BF16)
    o_ref[...] = jnp.dot(cs, w, preferred_element_type=F32) + b_ref[...]


def _modulation(c8, w_ada, b_ada):
    d, n = w_ada.shape
    tn = 1024
    return pl.pallas_call(
        _mod_kernel,
        grid=(n // tn,),
        in_specs=[
            pl.BlockSpec((8, d), lambda j: (0, 0)),
            pl.BlockSpec((d, tn), lambda j: (0, j)),
            pl.BlockSpec((1, tn), lambda j: (0, j)),
        ],
        out_specs=pl.BlockSpec((8, tn), lambda j: (0, j)),
        out_shape=jax.ShapeDtypeStruct((8, n), F32),
        compiler_params=pltpu.CompilerParams(
            dimension_semantics=("arbitrary",), vmem_limit_bytes=VMEM_LIMIT),
        name="adaln_mod",
    )(c8, w_ada, b_ada)


def _cast_weight_once(w_ref, wb_ref):
    first = (pl.program_id(0) == 0) & (pl.program_id(1) == 0)

    @pl.when(first)
    def _():
        def body(i, c):
            rows = pl.ds(pl.multiple_of(i * WEIGHT_CAST_ROWS, WEIGHT_CAST_ROWS), WEIGHT_CAST_ROWS)
            wb_ref[rows, :] = w_ref[rows, :].astype(BF16)
            return c
        lax.fori_loop(0, w_ref.shape[0] // WEIGHT_CAST_ROWS, body, 0)


def _rope_heads(t, cos, sin_signed, lane_lo):
    outs = []
    for hh in range(SEC // LANES):
        xh = t[:, hh * LANES:(hh + 1) * LANES]
        partner = jnp.where(lane_lo, pltpu.roll(xh, LANES - 32, 1), pltpu.roll(xh, 32, 1))
        outs.append(xh * cos + partner * sin_signed)
    return outs


def _proj_kernel(x_ref, pos_ref, scale_ref, shift_ref, ng_ref, w_ref, inv_ref,
                 lng_ref, lnb_ref, ws_ref, bs_ref,
                 ya_ref, qt_ref, k_ref, vt_ref, zb_ref, vln_ref, sv_ref, wb_ref, wsm_ref):
    tm = x_ref.shape[1]
    _cast_weight_once(w_ref, wb_ref)

    @pl.when((pl.program_id(0) == 0) & (pl.program_id(1) == 0))
    def _():
        row_c = lax.broadcasted_iota(jnp.int32, (GM_WINDOW, GM_WINDOW), 0) // CHUNK
        col_c = lax.broadcasted_iota(jnp.int32, (GM_WINDOW, GM_WINDOW), 1) // CHUNK
        for g in range(GM_GROUPS):
            wsm_ref[g] = jnp.where(row_c >= col_c, ws_ref[g], 0.0).astype(BF16)

    pos = pos_ref[0, 0].astype(F32)
    angs = []
    for a in range(tm // LANES):
        col = jnp.broadcast_to(pos[a:a + 1, :], (LANES, LANES)).T
        angs.append(col * inv_ref[...])
    ang = jnp.concatenate(angs, axis=0)
    cos = jnp.cos(ang)
    sin = jnp.sin(ang)
    lane = lax.broadcasted_iota(jnp.int32, (1, LANES), 1)
    lane_lo = (lane % DIFF_HEAD_DIM) < (DIFF_HEAD_DIM // 2)
    sin_signed = jnp.where(lane_lo, -sin, sin)

    x = x_ref[0]
    ms = jnp.mean(x * x, axis=-1, keepdims=True)
    gain = ng_ref[...] * (1.0 + scale_ref[0])
    h = (x * lax.rsqrt(ms + EPS)) * gain + shift_ref[0]
    hb = h.astype(BF16)

    def proj(sec):
        return jnp.dot(hb, wb_ref[:, sec * SEC:(sec + 1) * SEC], preferred_element_type=F32)

    pv = proj(1)
    pu = proj(0)
    v = _gelu(pv)
    n_win = tm // GM_WINDOW
    for g in range(GM_GROUPS):
        cols = slice(g * LANES, (g + 1) * LANES)
        vg = v[:, cols]
        mu = jnp.mean(vg, axis=-1, keepdims=True)
        vc = vg - mu
        var = jnp.mean(vc * vc, axis=-1, keepdims=True)
        vn = vc * lax.rsqrt(var + EPS)
        vn = vn * lng_ref[:, cols] + lnb_ref[:, cols]
        vln_ref[:, cols] = vn.astype(BF16)
        vwin = jnp.concatenate(
            [vln_ref[w * GM_WINDOW:(w + 1) * GM_WINDOW, cols] for w in range(n_win)], axis=1)
        svg = jnp.dot(wsm_ref[g], vwin, preferred_element_type=F32) + bs_ref[g]
        for w in range(n_win):
            sv_ref[w * GM_WINDOW:(w + 1) * GM_WINDOW, cols] = svg[:, w * LANES:(w + 1) * LANES]
    pz = proj(2)
    gu = _gelu(pu)

    pq = proj(3)
    ya_ref[0] = (gu * sv_ref[...] * _silu(pz)).astype(BF16)
    pk = proj(4)
    qscale = (DIFF_HEAD_DIM ** -0.5) * LOG2E
    for hh, qh in enumerate(_rope_heads(pq, cos, sin_signed, lane_lo)):
        for sub in range(tm // TQ_ATT):
            blk = qh[sub * TQ_ATT:(sub + 1) * TQ_ATT, :] * qscale
            qt_ref[0, hh, sub] = blk.T.astype(BF16)
    pvd = proj(5)
    for hh, kh in enumerate(_rope_heads(pk, cos, sin_signed, lane_lo)):
        k_ref[0, hh] = kh.astype(BF16)
    pzb = proj(6)
    for hh in range(DIFF_HEADS):
        for sub in range(tm // TK_ATT):
            blk = pvd[sub * TK_ATT:(sub + 1) * TK_ATT, hh * LANES:(hh + 1) * LANES]
            vt_ref[0, hh, sub] = blk.T.astype(BF16)
    zb_ref[0] = _silu(pzb).astype(BF16)


def _projection(x, pos4, scale, shift, norm_g, w_in, inv128, ln_g, ln_b, ws, bs3):
    b, s, d = x.shape
    tm = TM_PROJ
    head_shape = jax.ShapeDtypeStruct((b, DIFF_HEADS, s, LANES), BF16)
    row_shape = jax.ShapeDtypeStruct((b, s, SEC), BF16)
    head_spec = pl.BlockSpec((1, DIFF_HEADS, tm, LANES), lambda bi, i: (bi, 0, i, 0))
    row_spec = pl.BlockSpec((1, tm, SEC), lambda bi, i: (bi, i, 0))
    vt_shape = jax.ShapeDtypeStruct((b, DIFF_HEADS, s // TK_ATT, DIFF_V_DIM, TK_ATT), BF16)
    vt_spec = pl.BlockSpec((1, DIFF_HEADS, tm // TK_ATT, DIFF_V_DIM, TK_ATT),
                           lambda bi, i: (bi, 0, i, 0, 0))
    qt_shape = jax.ShapeDtypeStruct((b, DIFF_HEADS, s // TQ_ATT, LANES, TQ_ATT), BF16)
    qt_spec = pl.BlockSpec((1, DIFF_HEADS, tm // TQ_ATT, LANES, TQ_ATT),
                           lambda bi, i: (bi, 0, i, 0, 0))
    const2 = lambda bi, i: (0, 0)
    const3 = lambda bi, i: (0, 0, 0)
    return pl.pallas_call(
        _proj_kernel,
        grid=(b, s // tm),
        in_specs=[
            pl.BlockSpec((1, tm, d), lambda bi, i: (bi, i, 0)),
            pl.BlockSpec((1, 1, tm // LANES, LANES), lambda bi, i: (bi, i, 0, 0)),
            pl.BlockSpec((1, 1, d), lambda bi, i: (bi, 0, 0)),
            pl.BlockSpec((1, 1, d), lambda bi, i: (bi, 0, 0)),
            pl.BlockSpec((1, d), const2),
            pl.BlockSpec((d, N_SEC * SEC), const2,
                         pipeline_mode=pl.Buffered(1)),
            pl.BlockSpec((1, LANES), const2),
            pl.BlockSpec((1, SEC), const2),
            pl.BlockSpec((1, SEC), const2),
            pl.BlockSpec((GM_GROUPS, GM_WINDOW, GM_WINDOW), const3),
            pl.BlockSpec((GM_GROUPS, GM_WINDOW, 1), const3),
        ],
        out_specs=[row_spec, qt_spec, head_spec, vt_spec, row_spec],
        out_shape=[row_shape, qt_shape, head_shape, vt_shape, row_shape],
        scratch_shapes=[pltpu.VMEM((tm, SEC), BF16), pltpu.VMEM((tm, SEC), F32),
                        pltpu.VMEM((d, N_SEC * SEC), BF16),
                        pltpu.VMEM((GM_GROUPS, GM_WINDOW, GM_WINDOW), BF16)],
        compiler_params=pltpu.CompilerParams(
            dimension_semantics=("arbitrary", "arbitrary"), vmem_limit_bytes=VMEM_LIMIT),
        name="in_proj_gmlp_rope",
    )(x, pos4, scale, shift, norm_g, w_in, inv128, ln_g, ln_b, ws, bs3)


def _attn_kernel(lam_init, qt_ref, k_ref, vt_ref, zb_ref, g_ref, lq1_ref, lk1_ref, lq2_ref,
                 lk2_ref, o_ref, q2_ref, acc_ref, m_ref, s_carry, p_carry, a_carry):
    s_len = k_ref.shape[2]
    tq, tk, ng = TQ_ATT, TK_ATT, G_ATT
    n_super = s_len // SUPER_ATT
    tiles_per_super = SUPER_ATT // tk

    lam = (jnp.exp(jnp.sum(lq1_ref[...] * lk1_ref[...], axis=-1, keepdims=True))
           - jnp.exp(jnp.sum(lq2_ref[...] * lk2_ref[...], axis=-1, keepdims=True))
           + lam_init)

    norm_gain = g_ref[...] * (1.0 - lam_init)
    first_feat = lax.broadcasted_iota(jnp.int32, (LANES, 1), 0) < DIFF_HEAD_DIM
    key_pos = lax.broadcasted_iota(jnp.int32, (tk, 2 * tq), 0)
    qry_pos = lax.broadcasted_iota(jnp.int32, (tk, 2 * tq), 1) % tq
    ones_rows = jnp.ones((ONES_ROWS, tk), BF16)

    def allowed(q_off):
        return key_pos // CHUNK <= (qry_pos + q_off) // CHUNK

    def scores(job):
        g, j, _ = job
        kt = k_ref[0, 0, pl.ds(pl.multiple_of(j * tk, tk), tk), :]
        return jnp.dot(kt, q2_ref[g], preferred_element_type=F32)

    def softmax_update(job, s):
        g, _, q_off = job
        if q_off is not None:
            s = jnp.where(allowed(q_off), s, -jnp.inf)
        m_prev = m_ref[g]
        m_new = jnp.maximum(m_prev, jnp.max(s, axis=0, keepdims=True))
        m_ref[g] = m_new
        return jnp.exp2(m_prev - m_new), jnp.exp2(s - m_new).astype(BF16)

    def accumulate(job, alpha, p):
        g, j, _ = job
        vte = jnp.concatenate([vt_ref[0, 0, j], ones_rows], axis=0)
        acc_ref[g] = alpha * acc_ref[g] + jnp.dot(vte, p, preferred_element_type=F32)

    def run_stream(jobs, lookahead, pending_job):
        s_cur = s_carry[...]
        pend = (pending_job, a_carry[...], p_carry[...])
        for i, job in enumerate(jobs):
            nxt = jobs[i + 1] if i + 1 < len(jobs) else lookahead
            s_nxt = scores(nxt) if nxt is not None else None
            alpha, p = softmax_update(job, s_cur)
            accumulate(*pend)
            pend = (job, alpha, p)
            s_cur = s_nxt
        if lookahead is None:
            accumulate(*pend)
        else:
            s_carry[...] = s_cur
            a_carry[...] = pend[1]
            p_carry[...] = pend[2]

    def load_queries(g, si):
        qt = qt_ref[0, 0, si * ng + g]
        zero = jnp.zeros_like(qt)
        q2_ref[g, :, :tq] = jnp.where(first_feat, qt, zero)
        q2_ref[g, :, tq:] = jnp.where(first_feat, zero, qt)

    def reset_chain(g):
        m_ref[g] = jnp.full(m_ref.shape[1:], -jnp.inf, F32)
        acc_ref[g] = jnp.zeros(acc_ref.shape[1:], F32)

    def finish_chain(g, si):
        acc = acc_ref[g]
        o_all = acc[:DIFF_V_DIM] * (1.0 / acc[DIFF_V_DIM:DIFF_V_DIM + 1])
        o = o_all[:, :tq] - lam * o_all[:, tq:]
        ms = jnp.mean(o * o, axis=0, keepdims=True)
        on = ((o * lax.rsqrt(ms + EPS)) * norm_gain).T
        rows = pl.ds(pl.multiple_of(si * SUPER_ATT, SUPER_ATT) + g * tq, tq)
        o_ref[0, rows, :] = (on * zb_ref[0, rows, :].astype(F32)).astype(BF16)

    def full_tiles(jj, c):
        j0 = jj * FULL_TILES_PER_ITER
        jobs = [(g, j0 + u, None) for u in range(FULL_TILES_PER_ITER) for g in range(ng)]
        run_stream(jobs, (0, j0 + FULL_TILES_PER_ITER, None),
                   (ng - 1, jnp.maximum(j0 - 1, 0), None))
        return c

    def super_block(si, carry):
        for g in range(ng):
            load_queries(g, si)
            reset_chain(g)
        s_carry[...] = scores((0, 0, None))
        a_carry[...] = jnp.ones(a_carry.shape, F32)
        p_carry[...] = jnp.zeros(p_carry.shape, BF16)
        n_full = si * (tiles_per_super // FULL_TILES_PER_ITER)
        lax.fori_loop(0, n_full, full_tiles, 0)

        j0 = si * tiles_per_super
        jobs = []
        for bt in range(tiles_per_super):
            for g in range(ng):
                q_off = g * tq - bt * tk
                if q_off + tq <= 0:
                    continue
                needs_mask = q_off // CHUNK < (tk - 1) // CHUNK
                jobs.append((g, j0 + bt, q_off if needs_mask else None))
        assert jobs[0][0] == 0 and jobs[-1][0] == ng - 1
        run_stream(jobs, None, (ng - 1, jnp.maximum(j0 - 1, 0), None))
        for g in range(ng):
            finish_chain(g, si)
        return carry

    lax.fori_loop(0, n_super, super_block, 0)


def _attention(qt, k, vt, zb, subln_g_col, lq1, lk1, lq2, lk2, lam_init):
    b, h, s, _ = k.shape
    head_spec = pl.BlockSpec((1, 1, s, LANES), lambda bi, hi: (bi, hi, 0, 0))
    qt_spec = pl.BlockSpec((1, 1, s // TQ_ATT, LANES, TQ_ATT), lambda bi, hi: (bi, hi, 0, 0, 0))
    vt_spec = pl.BlockSpec((1, 1, s // TK_ATT, DIFF_V_DIM, TK_ATT),
                           lambda bi, hi: (bi, hi, 0, 0, 0))
    col_spec = pl.BlockSpec((1, s, LANES), lambda bi, hi: (bi, 0, hi))
    vec64 = pl.BlockSpec((1, DIFF_HEAD_DIM), lambda bi, hi: (0, 0))
    return pl.pallas_call(
        functools.partial(_attn_kernel, lam_init),
        grid=(b, h),
        in_specs=[qt_spec, head_spec, vt_spec, col_spec,
                  pl.BlockSpec((DIFF_V_DIM, 1), lambda bi, hi: (0, 0)),
                  vec64, vec64, vec64, vec64],
        out_specs=col_spec,
        out_shape=jax.ShapeDtypeStruct((b, s, h * LANES), BF16),
        scratch_shapes=[pltpu.VMEM((G_ATT, LANES, 2 * TQ_ATT), BF16),
                        pltpu.VMEM((G_ATT, DIFF_V_DIM + ONES_ROWS, 2 * TQ_ATT), F32),
                        pltpu.VMEM((G_ATT, 1, 2 * TQ_ATT), F32),
                        pltpu.VMEM((TK_ATT, 2 * TQ_ATT), F32),
                        pltpu.VMEM((TK_ATT, 2 * TQ_ATT), BF16),
                        pltpu.VMEM((1, 2 * TQ_ATT), F32)],
        compiler_params=pltpu.CompilerParams(
            dimension_semantics=("arbitrary", "arbitrary"), vmem_limit_bytes=VMEM_LIMIT),
        name="diff_attention",
    )(qt, k, vt, zb, subln_g_col, lq1, lk1, lq2, lk2)


def _out_kernel(ya_hbm, yb_hbm, x_hbm, gate_ref, w_ref, fg_ref, o_ref, wb_ref, xbuf, xsem,
                yabuf, ybbuf, ysem):
    _cast_weight_once(w_ref, wb_ref)
    tm = o_ref.shape[1]
    n_i = pl.num_programs(1)
    n_steps = pl.num_programs(0) * n_i
    step = pl.program_id(0) * n_i + pl.program_id(1)

    def copies(t):
        slot = t % X_RING_SLOTS
        rows = pl.ds(pl.multiple_of((t % n_i) * tm, tm), tm)
        bi = t // n_i
        return (pltpu.make_async_copy(x_hbm.at[bi, rows, :], xbuf.at[slot], xsem.at[slot]),
                pltpu.make_async_copy(ya_hbm.at[bi, rows, :], yabuf.at[slot], ysem.at[0, slot]),
                pltpu.make_async_copy(yb_hbm.at[bi, rows, :], ybbuf.at[slot], ysem.at[1, slot]))

    @pl.when(step == 0)
    def _():
        for t0 in range(X_RING_SLOTS - 1):
            for cp in copies(t0):
                cp.start()

    @pl.when(step + X_RING_SLOTS - 1 < n_steps)
    def _():
        for cp in copies(step + X_RING_SLOTS - 1):
            cp.start()

    for cp in copies(step):
        cp.wait()
    x_ref = xbuf.at[step % X_RING_SLOTS]
    ya_ref = yabuf.at[step % X_RING_SLOTS]
    yb_ref = ybbuf.at[step % X_RING_SLOTS]
    n_split = tm // OUT_ROW_SPLIT
    ys = []
    for r in range(n_split):
        rows = slice(r * OUT_ROW_SPLIT, (r + 1) * OUT_ROW_SPLIT)
        y = jnp.dot(ya_ref[rows, :], wb_ref[:SEC, :], preferred_element_type=F32)
        ys.append(y + jnp.dot(yb_ref[rows, :], wb_ref[SEC:, :], preferred_element_type=F32))
    for r in range(n_split):
        rows = slice(r * OUT_ROW_SPLIT, (r + 1) * OUT_ROW_SPLIT)
        xo = x_ref[rows, :] + gate_ref[0] * ys[r]
        ms = jnp.mean(xo * xo, axis=-1, keepdims=True)
        o_ref[0, rows, :] = (xo * lax.rsqrt(ms + EPS)) * fg_ref[...]


def _out_projection(ya, yb, x, gate, w_out, final_g):
    b, s, d = x.shape
    tm = TM_OUT
    return pl.pallas_call(
        _out_kernel,
        grid=(b, s // tm),
        in_specs=[
            pl.BlockSpec(memory_space=pl.ANY),
            pl.BlockSpec(memory_space=pl.ANY),
            pl.BlockSpec(memory_space=pl.ANY),
            pl.BlockSpec((1, 1, d), lambda bi, i: (bi, 0, 0)),
            pl.BlockSpec((2 * SEC, d), lambda bi, i: (0, 0), pipeline_mode=pl.Buffered(1)),
            pl.BlockSpec((1, d), lambda bi, i: (0, 0)),
        ],
        out_specs=pl.BlockSpec((1, tm, d), lambda bi, i: (bi, i, 0)),
        out_shape=jax.ShapeDtypeStruct((b, s, d), F32),
        scratch_shapes=[pltpu.VMEM((2 * SEC, d), BF16),
                        pltpu.VMEM((X_RING_SLOTS, tm, d), F32),
                        pltpu.SemaphoreType.DMA((X_RING_SLOTS,)),
                        pltpu.VMEM((X_RING_SLOTS, tm, SEC), BF16),
                        pltpu.VMEM((X_RING_SLOTS, tm, SEC), BF16),
                        pltpu.SemaphoreType.DMA((2, X_RING_SLOTS))],
        compiler_params=pltpu.CompilerParams(
            dimension_semantics=("arbitrary", "arbitrary"), vmem_limit_bytes=VMEM_LIMIT),
        name="out_proj_residual_norm",
    )(ya, yb, x, gate, w_out, final_g)


def kernel(x, c, positions, w_ada, b_ada, norm_g, w_in, gm_ln_g, gm_ln_b, gm_ws, gm_bs,
           lam_q1, lam_k1, lam_q2, lam_k2, diff_subln_g, w_out, final_g):
    b, s, d = x.shape
    depth = w_ada.shape[0]
    half = DIFF_HEAD_DIM // 2
    inv = ROPE_THETA ** (-jnp.arange(half, dtype=F32) * 2.0 / DIFF_HEAD_DIM)
    inv128 = jnp.tile(inv, LANES // half)[None, :]
    pos4 = positions.reshape(b, s // TM_PROJ, TM_PROJ // LANES, LANES)
    c8 = jnp.zeros((8, d), F32).at[:b].set(c)
    assert depth == 1
    l = 0
    mod = _modulation(c8, w_ada[l], b_ada[l][None, :])[:b]
    shift = mod[:, None, :d]
    scale = mod[:, None, d:2 * d]
    gate = mod[:, None, 2 * d:]
    ya, q, k, vt, zb = _projection(
        x, pos4, scale, shift, norm_g[l][None, :], w_in[l], inv128,
        gm_ln_g[l][None, :], gm_ln_b[l][None, :], gm_ws[l], gm_bs[l][:, :, None])
    lam_init = 0.8 - 0.6 * math.exp(-0.3 * l)
    yb = _attention(q, k, vt, zb, diff_subln_g[l][:, None], lam_q1[l][None, :],
                    lam_k1[l][None, :], lam_q2[l][None, :], lam_k2[l][None, :], lam_init)
    return _out_projection(ya, yb, x, gate, w_out[l], final_g[None, :])
```
